```python
import math
import jax, jax.numpy as jnp
from jax import lax
import numpy as np

D_MODEL = 1024
BATCH = 8
SEQ = 4096
DEPTH = 2

N_HEADS = 8
HEAD_DIM = 64
ROPE_THETA = 10000.0
Q_BLOCK = 128
SSM_WIDTH = 512
SSM_GROUP = 16
SSM_GROUPS = SSM_WIDTH // SSM_GROUP
SSM_STATE = 64
DT_MIN = 1e-3
DT_MAX = 1e-1
D_FF = 2816
CONV_WIDTH = 3
EPS = 1e-6
SUBLN_EPS = 1e-5

QK_WIDTH = 2 * N_HEADS * HEAD_DIM
V_WIDTH = N_HEADS * 2 * HEAD_DIM
SPLIT_IDX = [QK_WIDTH, 2 * QK_WIDTH, 2 * QK_WIDTH + V_WIDTH,
             2 * QK_WIDTH + V_WIDTH + SSM_WIDTH, 2 * QK_WIDTH + V_WIDTH + SSM_WIDTH + D_MODEL]
IN_WIDTH = 2 * QK_WIDTH + V_WIDTH + SSM_WIDTH + 2 * D_MODEL

kernel_name = "hybrid_diffattn_s5_convffn_adaln_encoder"


def rmsnorm(x, w, eps=EPS):
    xf = x.astype(jnp.float32)
    y = xf * lax.rsqrt(jnp.mean(xf * xf, axis=-1, keepdims=True) + eps)
    return (y * w.astype(jnp.float32)).astype(x.dtype)


def rope_tables(seq):
    half = HEAD_DIM // 2
    inv_freq = ROPE_THETA ** (-jnp.arange(0, half, dtype=jnp.float32) / half)
    ang = jnp.arange(seq, dtype=jnp.float32)[:, None] * inv_freq[None, :]
    return jnp.cos(ang)[:, None, :], jnp.sin(ang)[:, None, :]


def apply_rope(x, cos, sin):
    half = HEAD_DIM // 2
    x1, x2 = x[..., :half], x[..., half:]
    out = jnp.concatenate([x1 * cos - x2 * sin, x2 * cos + x1 * sin], axis=-1)
    return out.astype(x.dtype)


def diff_attention(q, k, v, lam, sub_w, lambda_init):
    B, S = q.shape[0], q.shape[1]
    nb = S // Q_BLOCK
    scale = 1.0 / math.sqrt(HEAD_DIM)
    qb = q.reshape(B, nb, Q_BLOCK, 2 * N_HEADS, HEAD_DIM).transpose(1, 0, 2, 3, 4)

    def attend_block(qblk):
        s = jnp.einsum('bqhd,bkhd->bhqk', qblk, k).astype(jnp.float32) * scale
        p = jax.nn.softmax(s, axis=-1).reshape(B, N_HEADS, 2, Q_BLOCK, S)
        a = p[:, :, 0] - lam * p[:, :, 1]
        return jnp.einsum('bhqk,bkhe->bqhe', a.astype(v.dtype), v)

    o = lax.map(attend_block, qb)
    o = o.transpose(1, 0, 2, 3, 4).reshape(B, S, N_HEADS, 2 * HEAD_DIM)
    o = rmsnorm(o, sub_w, SUBLN_EPS) * (1.0 - lambda_init)
    return o.reshape(B, S, V_WIDTH)


def s5_direction(u, lam_re, lam_im, log_dt, b_re, b_im, c_re, c_im, reverse):
    S = u.shape[0]
    lam_re = lam_re.astype(jnp.float32)
    lam_im = lam_im.astype(jnp.float32)
    dt = jnp.exp(log_dt.astype(jnp.float32))[:, None]
    mag = jnp.exp(lam_re * dt)
    ang = lam_im * dt
    abar_re = mag * jnp.cos(ang)
    abar_im = mag * jnp.sin(ang)
    den = lam_re * lam_re + lam_im * lam_im
    nr = abar_re - 1.0
    ni = abar_im
    f_re = (nr * lam_re + ni * lam_im) / den
    f_im = (ni * lam_re - nr * lam_im) / den
    b_re = b_re.astype(jnp.float32)
    b_im = b_im.astype(jnp.float32)
    bb_re = f_re[..., None] * b_re - f_im[..., None] * b_im
    bb_im = f_re[..., None] * b_im + f_im[..., None] * b_re
    bu_re = jnp.einsum('sbgn,gpn->sbgp', u, bb_re)
    bu_im = jnp.einsum('sbgn,gpn->sbgp', u, bb_im)
    G, P = abar_re.shape
    a_re = jnp.broadcast_to(abar_re[None, None], (S, 1, G, P))
    a_im = jnp.broadcast_to(abar_im[None, None], (S, 1, G, P))

    def combine(e1, e2):
        a1r, a1i, b1r, b1i = e1
        a2r, a2i, b2r, b2i = e2
        return (a2r * a1r - a2i * a1i,
                a2r * a1i + a2i * a1r,
                a2r * b1r - a2i * b1i + b2r,
                a2r * b1i + a2i * b1r + b2i)

    _, _, xr, xi = lax.associative_scan(combine, (a_re, a_im, bu_re, bu_im), axis=0, reverse=reverse)
    return (jnp.einsum('sbgp,gnp->sbgn', xr, c_re.astype(jnp.float32))
            - jnp.einsum('sbgp,gnp->sbgn', xi, c_im.astype(jnp.float32)))


def bidirectional_s5(u, lam_re, lam_im, log_dt, b_re, b_im, c_re, c_im, d_skip):
    B, S, _ = u.shape
    us = u.astype(jnp.float32).transpose(1, 0, 2).reshape(S, B, SSM_GROUPS, SSM_GROUP)
    y_fwd = s5_direction(us, lam_re[0], lam_im[0], log_dt[0], b_re[0], b_im[0], c_re[0], c_im[0], False)
    y_bwd = s5_direction(us, lam_re[1], lam_im[1], log_dt[1], b_re[1], b_im[1], c_re[1], c_im[1], True)
    y = y_fwd + y_bwd + d_skip.astype(jnp.float32).reshape(SSM_GROUPS, SSM_GROUP) * us
    return y.reshape(S, B, SSM_WIDTH).transpose(1, 0, 2).astype(u.dtype)


def depthwise_conv_centred(x, w, b):
    xp = jnp.pad(x, ((0, 0), (1, 1), (0, 0)))
    return xp[:, :-2] * w[0] + xp[:, 1:-1] * w[1] + xp[:, 2:] * w[2] + b


def setup_inputs(seed: int = 0) -> dict:
    key = jax.random.key(seed)
    ks = jax.random.split(key, 32)
    f32 = jnp.float32
    L, G, P, N = DEPTH, SSM_GROUPS, SSM_STATE, SSM_GROUP

    def nrm(k, shape, scale):
        return jax.random.normal(k, shape, f32) * scale

    n_idx = jnp.arange(P, dtype=f32)
    lam_re = -0.5 + 0.01 * jax.random.normal(ks[14], (L, 2, G, P), f32)
    lam_im = math.pi * n_idx + 0.01 * jax.random.normal(ks[15], (L, 2, G, P), f32)
    log_dt = jax.random.uniform(ks[16], (L, 2, G), f32, math.log(DT_MIN), math.log(DT_MAX))
    inv_sqrt2 = 1.0 / math.sqrt(2.0)

    return {
        "x": nrm(ks[0], (BATCH, SEQ, D_MODEL), 1.0),
        "c": nrm(ks[1], (BATCH, D_MODEL), 1.0),
        "norm1_w": 1.0 + nrm(ks[2], (L, D_MODEL), 0.01),
        "norm2_w": 1.0 + nrm(ks[3], (L, D_MODEL), 0.01),
        "ada_w": nrm(ks[4], (L, D_MODEL, 6 * D_MODEL), D_MODEL ** -0.5),
        "ada_b": nrm(ks[5], (L, 6 * D_MODEL), 0.01),
        "w_in": nrm(ks[6], (L, D_MODEL, IN_WIDTH), D_MODEL ** -0.5),
        "lam_q1": nrm(ks[7], (L, HEAD_DIM), 0.1),
        "lam_k1": nrm(ks[8], (L, HEAD_DIM), 0.1),
        "lam_q2": nrm(ks[9], (L, HEAD_DIM), 0.1),
        "lam_k2": nrm(ks[10], (L, HEAD_DIM), 0.1),
        "subln_w": 1.0 + nrm(ks[11], (L, 2 * HEAD_DIM), 0.01),
        "w_attn_out": nrm(ks[12], (L, V_WIDTH, D_MODEL), V_WIDTH ** -0.5),
        "ssm_lam_re": lam_re,
        "ssm_lam_im": lam_im,
        "ssm_log_dt": log_dt,
        "ssm_b_re": nrm(ks[17], (L, 2, G, P, N), N ** -0.5 * inv_sqrt2),
        "ssm_b_im": nrm(ks[18], (L, 2, G, P, N), N ** -0.5 * inv_sqrt2),
        "ssm_c_re": nrm(ks[19], (L, 2, G, N, P), P ** -0.5 * inv_sqrt2),
        "ssm_c_im": nrm(ks[20], (L, 2, G, N, P), P ** -0.5 * inv_sqrt2),
        "ssm_d": nrm(ks[21], (L, SSM_WIDTH), 1.0),
        "w_glu_a": nrm(ks[22], (L, SSM_WIDTH, D_MODEL), SSM_WIDTH ** -0.5),
        "w_glu_b": nrm(ks[23], (L, SSM_WIDTH, D_MODEL), SSM_WIDTH ** -0.5),
        "w_out": nrm(ks[24], (L, D_MODEL, D_MODEL), D_MODEL ** -0.5),
        "w_up": nrm(ks[25], (L, D_MODEL, 2 * D_FF), D_MODEL ** -0.5),
        "conv_w": nrm(ks[26], (L, CONV_WIDTH, 2 * D_FF), CONV_WIDTH ** -0.5),
        "conv_b": nrm(ks[27], (L, 2 * D_FF), 0.01),
        "w_down": nrm(ks[28], (L, D_FF, D_MODEL), D_FF ** -0.5),
        "final_w": 1.0 + nrm(ks[29], (D_MODEL,), 0.01),
    }


def reference(x, c, norm1_w, norm2_w, ada_w, ada_b, w_in, lam_q1, lam_k1, lam_q2, lam_k2, subln_w,
              w_attn_out, ssm_lam_re, ssm_lam_im, ssm_log_dt, ssm_b_re, ssm_b_im, ssm_c_re, ssm_c_im,
              ssm_d, w_glu_a, w_glu_b, w_out, w_up, conv_w, conv_b, w_down, final_w):
    B, S, _ = x.shape
    cos, sin = rope_tables(S)
    c_act = jax.nn.silu(c)
    for l in range(DEPTH):
        lambda_init = 0.8 - 0.6 * math.exp(-0.3 * l)
        mod = c_act @ ada_w[l] + ada_b[l]
        shift1, scale1, gate1, shift2, scale2, gate2 = [m[:, None, :] for m in jnp.split(mod, 6, axis=-1)]

        h = rmsnorm(x, norm1_w[l]) * (1.0 + scale1) + shift1
        proj = h @ w_in[l]
        q, k, v, u, g_attn, g_ssm = jnp.split(proj, SPLIT_IDX, axis=-1)

        q = apply_rope(q.reshape(B, S, 2 * N_HEADS, HEAD_DIM), cos, sin)
        k = apply_rope(k.reshape(B, S, 2 * N_HEADS, HEAD_DIM), cos, sin)
        v = v.reshape(B, S, N_HEADS, 2 * HEAD_DIM)
        lam = (jnp.exp(jnp.sum(lam_q1[l] * lam_k1[l]).astype(jnp.float32))
               - jnp.exp(jnp.sum(lam_q2[l] * lam_k2[l]).astype(jnp.float32)) + lambda_init)
        o_attn = diff_attention(q, k, v, lam, subln_w[l], lambda_init) @ w_attn_out[l]

        y = bidirectional_s5(u, ssm_lam_re[l], ssm_lam_im[l], ssm_log_dt[l], ssm_b_re[l], ssm_b_im[l],
                             ssm_c_re[l], ssm_c_im[l], ssm_d[l])
        y = jax.nn.gelu(y)
        o_ssm = (y @ w_glu_a[l]) * jax.nn.sigmoid(y @ w_glu_b[l])

        merged = jax.nn.sigmoid(g_attn) * o_attn + jax.nn.sigmoid(g_ssm) * o_ssm
        x = x + gate1 * (merged @ w_out[l])

        h = rmsnorm(x, norm2_w[l]) * (1.0 + scale2) + shift2
        up = depthwise_conv_centred(h @ w_up[l], conv_w[l], conv_b[l])
        val, gt = jnp.split(up, 2, axis=-1)
        x = x + gate2 * ((jax.nn.silu(gt) * val) @ w_down[l])
    return rmsnorm(x, final_w)
```

```python
import functools
import math

import jax
import jax.numpy as jnp
from jax import lax
from jax.experimental import pallas as pl
from jax.experimental.pallas import tpu as pltpu

N_HEADS = 8
HEAD_DIM = 64
HEAD_WIDTH = 2 * HEAD_DIM
ROPE_THETA = 10000.0
SSM_GROUP = 16
SSM_STATE = 64
CONV_WIDTH = 3
EPS = 1e-6
SUBLN_EPS = 1e-5

LANES = 128
SUBLANES = 8
MXU_DIM = 256
VMEM_LIMIT_BYTES = 56 * 1024 * 1024

BF16 = jnp.bfloat16
F32 = jnp.float32


def _params(semantics):
    return pltpu.CompilerParams(dimension_semantics=semantics, vmem_limit_bytes=VMEM_LIMIT_BYTES)


def _modulated_rmsnorm(x, w, scale, shift):
    y = x * lax.rsqrt(jnp.mean(x * x, axis=-1, keepdims=True) + EPS)
    return (y * w) * (1.0 + scale) + shift


def _adaln_kernel(c_ref, w_ref, b_ref, o_ref):
    c = c_ref[...]
    c = c * jax.nn.sigmoid(c)
    w = w_ref[...]
    c_hi = c.astype(BF16)
    c_lo = (c - c_hi.astype(F32)).astype(BF16)
    w_hi = w.astype(BF16)
    w_lo = (w - w_hi.astype(F32)).astype(BF16)
    acc = jnp.dot(c_hi, w_hi, preferred_element_type=F32)
    acc += jnp.dot(c_hi, w_lo, preferred_element_type=F32)
    acc += jnp.dot(c_lo, w_hi, preferred_element_type=F32)
    o_ref[...] = acc + b_ref[...]


def _adaln(c, ada_w, ada_b):
    L, D, N = ada_w.shape
    B = c.shape[0]
    tn = 1536 if N % 1536 == 0 else N
    return pl.pallas_call(
        _adaln_kernel,
        grid=(L, N // tn),
        in_specs=[
            pl.BlockSpec((B, D), lambda l, j: (0, 0)),
            pl.BlockSpec((None, D, tn), lambda l, j: (l, 0, j)),
            pl.BlockSpec((None, 1, tn), lambda l, j: (l, 0, j)),
        ],
        out_specs=pl.BlockSpec((None, B, tn), lambda l, j: (l, 0, j)),
        out_shape=jax.ShapeDtypeStruct((L, B, N), F32),
        compiler_params=_params(("arbitrary", "arbitrary")),
        name="adaln",
    )(c, ada_w, ada_b.reshape(L, 1, N))


def _inproj_kernel(x_ref, mod_ref, nw_ref, w_ref, cos_ref, sin_ref,
                   q_ref, k_ref, v_ref, u_ref, ga_ref, gs_ref, *, d_model, ssm_width):
    x = x_ref[...]
    h = _modulated_rmsnorm(x, nw_ref[...], mod_ref[1:2, :], mod_ref[0:1, :]).astype(BF16)
    cos = cos_ref[...]
    sin = sin_ref[...]
    first_half = (lax.broadcasted_iota(jnp.int32, cos.shape, 1) % HEAD_DIM) < (HEAD_DIM // 2)
    qk_width = N_HEADS * HEAD_WIDTH

    def rope_heads(col0, out_ref, scale):
        for hd in range(N_HEADS):
            c0 = col0 + hd * HEAD_WIDTH
            t = jnp.dot(h, w_ref[:, c0:c0 + HEAD_WIDTH], preferred_element_type=F32)
            partner = jnp.where(first_half,
                                pltpu.roll(t, HEAD_WIDTH - HEAD_DIM // 2, axis=1),
                                pltpu.roll(t, HEAD_DIM // 2, axis=1))
            r = t * cos + partner * sin
            if scale != 1.0:
                r = r * scale
            out_ref[hd] = r.astype(BF16)

    rope_heads(0, q_ref, 1.0 / math.sqrt(HEAD_DIM))
    rope_heads(qk_width, k_ref, 1.0)
    for hd in range(N_HEADS):
        c0 = 2 * qk_width + hd * HEAD_WIDTH
        v_ref[hd] = jnp.dot(h, w_ref[:, c0:c0 + HEAD_WIDTH], preferred_element_type=F32).astype(BF16)
    c0 = 3 * qk_width
    u_ref[...] = jnp.dot(h, w_ref[:, c0:c0 + ssm_width], preferred_element_type=F32).astype(BF16)
    c0 += ssm_width
    ga_ref[...] = jax.nn.sigmoid(
        jnp.dot(h, w_ref[:, c0:c0 + d_model], preferred_element_type=F32)).astype(BF16)
    c0 += d_model
    gs_ref[...] = jax.nn.sigmoid(
        jnp.dot(h, w_ref[:, c0:c0 + d_model], preferred_element_type=F32)).astype(BF16)


def _inproj(x, mod, norm_w, w_in, cos, sin, ssm_width, tm):
    B, S, D = x.shape
    n_in = w_in.shape[1]
    head_shape = jax.ShapeDtypeStruct((B, N_HEADS, S, HEAD_WIDTH), BF16)
    head_spec = pl.BlockSpec((None, N_HEADS, tm, HEAD_WIDTH), lambda b, s: (b, 0, s, 0))
    row_spec = pl.BlockSpec((None, tm, D), lambda b, s: (b, s, 0))
    return pl.pallas_call(
        functools.partial(_inproj_kernel, d_model=D, ssm_width=ssm_width),
        grid=(B, S // tm),
        in_specs=[
            row_spec,
            pl.BlockSpec((None, 6, D), lambda b, s: (b, 0, 0)),
            pl.BlockSpec((1, D), lambda b, s: (0, 0)),
            pl.BlockSpec((D, n_in), lambda b, s: (0, 0)),
            pl.BlockSpec((tm, LANES), lambda b, s: (s, 0)),
            pl.BlockSpec((tm, LANES), lambda b, s: (s, 0)),
        ],
        out_specs=[
            head_spec, head_spec, head_spec,
            pl.BlockSpec((tm, ssm_width), lambda b, s: (s, b)),
            row_spec, row_spec,
        ],
        out_shape=[
            head_shape, head_shape, head_shape,
            jax.ShapeDtypeStruct((S, B * ssm_width), BF16),
            jax.ShapeDtypeStruct((B, S, D), BF16),
            jax.ShapeDtypeStruct((B, S, D), BF16),
        ],
        compiler_params=_params(("arbitrary", "arbitrary")),
        name="inproj",
    )(x, mod, norm_w.reshape(1, D), w_in, cos, sin)


def _attn_kernel(lam_ref, q_ref, k_ref, v_ref, sw_ref, o_ref, vaug_ref, *, tk, out_scale):
    S = k_ref.shape[0]

    @pl.when(pl.program_id(2) == 0)
    def _():
        vaug_ref[:, :HEAD_WIDTH] = v_ref[...]
        ones_col = lax.broadcasted_iota(jnp.int32, (S, MXU_DIM - HEAD_WIDTH), 1) == 0
        vaug_ref[:, HEAD_WIDTH:] = ones_col.astype(BF16)

    q = q_ref[...]
    lane = lax.broadcasted_iota(jnp.int32, q.shape, 1)
    zero = jnp.zeros_like(q)
    subs = []
    for q_sub in (jnp.where(lane < HEAD_DIM, q, zero), jnp.where(lane >= HEAD_DIM, q, zero)):
        m = jnp.full((q.shape[0], 1), -jnp.inf, F32)
        acc = jnp.zeros((q.shape[0], MXU_DIM), F32)
        for c in range(S // tk):
            kc = k_ref[c * tk:(c + 1) * tk, :]
            s = lax.dot_general(q_sub, kc, (((1,), (1,)), ((), ())), preferred_element_type=F32)
            m_new = jnp.maximum(m, jnp.max(s, axis=1, keepdims=True))
            alpha = jnp.exp(m - m_new)
            p = jnp.exp(s - m_new).astype(BF16)
            acc = acc * alpha + jnp.dot(p, vaug_ref[c * tk:(c + 1) * tk, :], preferred_element_type=F32)
            m = m_new
        subs.append(acc[:, :HEAD_WIDTH] / acc[:, HEAD_WIDTH:HEAD_WIDTH + 1])
    o = subs[0] - lam_ref[0] * subs[1]
    o = o * lax.rsqrt(jnp.mean(o * o, axis=-1, keepdims=True) + SUBLN_EPS)
    o_ref[...] = ((o * sw_ref[...]) * out_scale).astype(BF16)


def _attention(q, k, v, lam, sub_w, out_scale, tq, tk):
    B, H, S, W = q.shape
    kv_spec = pl.BlockSpec((None, None, S, W), lambda b, h, i: (b, h, 0, 0))
    return pl.pallas_call(
        functools.partial(_attn_kernel, tk=tk, out_scale=out_scale),
        grid=(B, H, S // tq),
        in_specs=[
            pl.BlockSpec(memory_space=pltpu.SMEM),
            pl.BlockSpec((None, None, tq, W), lambda b, h, i: (b, h, i, 0)),
            kv_spec, kv_spec,
            pl.BlockSpec((1, W), lambda b, h, i: (0, 0)),
        ],
        out_specs=pl.BlockSpec((None, tq, W), lambda b, h, i: (b, i, h)),
        out_shape=jax.ShapeDtypeStruct((B, S, H * W), BF16),
        scratch_shapes=[pltpu.VMEM((S, MXU_DIM), BF16)],
        compiler_params=_params(("arbitrary", "arbitrary", "arbitrary")),
        name="diff_attn",
    )(lam.reshape(1), q, k, v, sub_w.reshape(1, W))


def _s5_discretize(lam_re, lam_im, log_dt, b_re, b_im, c_re, c_im, batch):
    G, P = lam_re.shape
    N = b_re.shape[-1]
    dt = jnp.exp(log_dt)[:, None]
    mag = jnp.exp(lam_re * dt)
    ang = lam_im * dt
    abar_re = mag * jnp.cos(ang)
    abar_im = mag * jnp.sin(ang)
    den = lam_re * lam_re + lam_im * lam_im
    nr = abar_re - 1.0
    ni = abar_im
    f_re = (nr * lam_re + ni * lam_im) / den
    f_im = (ni * lam_re - nr * lam_im) / den
    bb_re = f_re[..., None] * b_re - f_im[..., None] * b_im
    bb_im = f_re[..., None] * b_im + f_im[..., None] * b_re
    gh = G // 2
    eye = jnp.eye(gh, dtype=F32)

    def in_block(bb):
        return jnp.einsum('gpn,gk->gnkp', bb, eye).reshape(gh * N, gh * P)

    def out_block(cc):
        return jnp.einsum('gnp,gk->gpkn', cc, eye).reshape(gh * P, gh * N)

    b_mat = jnp.stack([jnp.concatenate([in_block(bb_re[h * gh:(h + 1) * gh]),
                                        in_block(bb_im[h * gh:(h + 1) * gh])], axis=1) for h in range(2)])
    c_mat = jnp.stack([jnp.concatenate([out_block(c_re[h * gh:(h + 1) * gh]),
                                        out_block(-c_im[h * gh:(h + 1) * gh])], axis=0) for h in range(2)])
    a = jnp.stack([abar_re.reshape(G * P), abar_im.reshape(G * P)])
    a = jnp.broadcast_to(a[:, None, :], (2, batch, G * P))
    return b_mat.astype(BF16), c_mat.astype(BF16), a


def _s5_kernel(uf_ref, ub_ref, bf_ref, bb_ref, cf_ref, cb_ref, af_ref, ab_ref,
               yf_ref, yb_ref, xf_ref, xb_ref, carf_ref, carb_ref, *, steps, batch, lane_chunk):
    half_in = uf_ref.shape[1] // 2
    half_state = af_ref.shape[2] // 2

    @pl.when(pl.program_id(0) == 0)
    def _():
        carf_ref[...] = jnp.zeros_like(carf_ref)
        carb_ref[...] = jnp.zeros_like(carb_ref)

    for h in range(2):
        cols = slice(2 * half_state * h, 2 * half_state * (h + 1))
        xf_ref[:, cols] = jnp.dot(uf_ref[:, half_in * h:half_in * (h + 1)], bf_ref[h],
                                  preferred_element_type=F32)
        xb_ref[:, cols] = jnp.dot(ub_ref[:, half_in * h:half_in * (h + 1)], bb_ref[h],
                                  preferred_element_type=F32)

    for h in range(2):
        for j in range(half_state // lane_chunk):
            re = slice(2 * half_state * h + lane_chunk * j, 2 * half_state * h + lane_chunk * (j + 1))
            im = slice(re.start + half_state, re.stop + half_state)
            ac = slice(half_state * h + lane_chunk * j, half_state * h + lane_chunk * (j + 1))
            arf, aif = af_ref[0, :, ac], af_ref[1, :, ac]
            arb, aib = ab_ref[0, :, ac], ab_ref[1, :, ac]

            def body(t, carry):
                fr, fi, br, bi = carry
                rf = pl.ds(pl.multiple_of(t * batch, batch), batch)
                rb = pl.ds(pl.multiple_of((steps - 1 - t) * batch, batch), batch)
                nfr = arf * fr - aif * fi + xf_ref[rf, re]
                nfi = arf * fi + aif * fr + xf_ref[rf, im]
                nbr = arb * br - aib * bi + xb_ref[rb, re]
                nbi = arb * bi + aib * br + xb_ref[rb, im]
                xf_ref[rf, re] = nfr
                xf_ref[rf, im] = nfi
                xb_ref[rb, re] = nbr
                xb_ref[rb, im] = nbi
                return nfr, nfi, nbr, nbi

            fr, fi, br, bi = lax.fori_loop(
                0, steps, body, (carf_ref[:, re], carf_ref[:, im], carb_ref[:, re], carb_ref[:, im]),
                unroll=8)
            carf_ref[:, re] = fr
            carf_ref[:, im] = fi
            carb_ref[:, re] = br
            carb_ref[:, im] = bi

    for h in range(2):
        cols = slice(2 * half_state * h, 2 * half_state * (h + 1))
        yf_ref[:, half_in * h:half_in * (h + 1)] = jnp.dot(
            xf_ref[:, cols].astype(BF16), cf_ref[h], preferred_element_type=F32)
        yb_ref[:, half_in * h:half_in * (h + 1)] = jnp.dot(
            xb_ref[:, cols].astype(BF16), cb_ref[h], preferred_element_type=F32)


def _s5(u_tm, fwd, bwd, batch, steps):
    rows_total, W = u_tm.shape
    rows = steps * batch
    n = rows_total // rows
    (bf, cf, af), (bb, cb, ab) = fwd, bwd
    n_state = af.shape[2]

    def full(arr):
        return pl.BlockSpec(arr.shape, lambda i: (0,) * arr.ndim)

    return pl.pallas_call(
        functools.partial(_s5_kernel, steps=steps, batch=batch, lane_chunk=512),
        grid=(n,),
        in_specs=[
            pl.BlockSpec((rows, W), lambda i: (i, 0)),
            pl.BlockSpec((rows, W), lambda i: (n - 1 - i, 0)),
            full(bf), full(bb), full(cf), full(cb), full(af), full(ab),
        ],
        out_specs=[
            pl.BlockSpec((rows, W), lambda i: (i, 0)),
            pl.BlockSpec((rows, W), lambda i: (n - 1 - i, 0)),
        ],
        out_shape=[jax.ShapeDtypeStruct((rows_total, W), F32)] * 2,
        scratch_shapes=[
            pltpu.VMEM((rows, 2 * n_state), F32), pltpu.VMEM((rows, 2 * n_state), F32),
            pltpu.VMEM((batch, 2 * n_state), F32), pltpu.VMEM((batch, 2 * n_state), F32),
        ],
        compiler_params=_params(("arbitrary",)),
        name="s5_scan",
    )(u_tm, u_tm, bf, bb, cf, cb, af, ab)


def _gelu_tanh(y):
    return 0.5 * y * (1.0 + jnp.tanh(math.sqrt(2.0 / math.pi) * (y + 0.044715 * (y * y * y))))


def _merge_kernel(x_ref, mod_ref, oa_ref, yf_ref, yb_ref, u_ref, d_ref, ga_ref, gs_ref,
                  wao_ref, wglu_ref, wout_ref, o_ref, *, d_model):
    y = yf_ref[...] + yb_ref[...] + d_ref[...] * u_ref[...].astype(F32)
    y = _gelu_tanh(y).astype(BF16)
    ab = jnp.dot(y, wglu_ref[...], preferred_element_type=F32)
    o_ssm = ab[:, :d_model] * jax.nn.sigmoid(ab[:, d_model:])
    o_attn = jnp.dot(oa_ref[...], wao_ref[...], preferred_element_type=F32)
    merged = ga_ref[...].astype(F32) * o_attn + gs_ref[...].astype(F32) * o_ssm
    out = jnp.dot(merged.astype(BF16), wout_ref[...], preferred_element_type=F32)
    o_ref[...] = x_ref[...] + mod_ref[2:3, :] * out


def _merge(x, mod, o_attn, yf, yb, u_tm, ssm_d, ga, gs, w_attn_out, w_glu, w_out, tm):
    B, S, D = x.shape
    W = ssm_d.shape[0]
    row_spec = pl.BlockSpec((None, tm, D), lambda b, s: (b, s, 0))
    tm_spec = pl.BlockSpec((tm, W), lambda b, s: (s, b))

    def full(arr):
        return pl.BlockSpec(arr.shape, lambda b, s: (0,) * arr.ndim)

    return pl.pallas_call(
        functools.partial(_merge_kernel, d_model=D),
        grid=(B, S // tm),
        in_specs=[
            row_spec,
            pl.BlockSpec((None, 6, D), lambda b, s: (b, 0, 0)),
            row_spec, tm_spec, tm_spec, tm_spec,
            pl.BlockSpec((1, W), lambda b, s: (0, 0)),
            row_spec, row_spec,
            full(w_attn_out), full(w_glu), full(w_out),
        ],
        out_specs=row_spec,
        out_shape=jax.ShapeDtypeStruct((B, S, D), F32),
        compiler_params=_params(("arbitrary", "arbitrary")),
        name="merge",
    )(x, mod, o_attn, yf.reshape(S, B * W), yb.reshape(S, B * W), u_tm, ssm_d.reshape(1, W), ga, gs,
      w_attn_out, w_glu, w_out)


def _ffn_kernel(x_ref, xp_ref, xn_ref, mod_ref, nw_ref, wup_ref, cw_ref, cb_ref, wdn_ref, fw_ref,
                o_ref, up_ref, act_ref, *, d_ff, ff_chunk, final_norm):
    tm = x_ref.shape[0]
    si = pl.program_id(1)
    keep_prev = (si > 0).astype(F32)
    keep_next = (si < pl.num_programs(1) - 1).astype(F32)
    nw, scale, shift = nw_ref[...], mod_ref[4:5, :], mod_ref[3:4, :]
    x = x_ref[...]
    h = jnp.concatenate([
        _modulated_rmsnorm(xp_ref[...], nw, scale, shift) * keep_prev,
        _modulated_rmsnorm(x, nw, scale, shift),
        _modulated_rmsnorm(xn_ref[...], nw, scale, shift) * keep_next,
    ], axis=0).astype(BF16)

    def conv(col0):
        up_ref[...] = jnp.dot(h, wup_ref[:, col0:col0 + ff_chunk], preferred_element_type=F32)
        cw = cw_ref[:, col0:col0 + ff_chunk]
        return (up_ref[SUBLANES - 1:SUBLANES - 1 + tm, :] * cw[0:1, :]
                + up_ref[SUBLANES:SUBLANES + tm, :] * cw[1:2, :]
                + up_ref[SUBLANES + 1:SUBLANES + 1 + tm, :] * cw[2:3, :]
                + cb_ref[:, col0:col0 + ff_chunk])

    for c in range(d_ff // ff_chunk):
        val = conv(c * ff_chunk)
        gt = conv(d_ff + c * ff_chunk)
        act_ref[:, c * ff_chunk:(c + 1) * ff_chunk] = ((gt * jax.nn.sigmoid(gt)) * val).astype(BF16)

    out = jnp.dot(act_ref[...], wdn_ref[...], preferred_element_type=F32)
    xo = x + mod_ref[5:6, :] * out
    if final_norm:
        xo = (xo * lax.rsqrt(jnp.mean(xo * xo, axis=-1, keepdims=True) + EPS)) * fw_ref[...]
    o_ref[...] = xo


def _ffn(x, mod, norm_w, w_up, conv_w, conv_b, w_down, final_w, final_norm, tm):
    B, S, D = x.shape
    d_ff = w_down.shape[0]
    ff_chunk = MXU_DIM if d_ff % MXU_DIM == 0 else d_ff
    halo_blocks = tm // SUBLANES
    n_halo = S // SUBLANES
    row_spec = pl.BlockSpec((None, tm, D), lambda b, s: (b, s, 0))

    def full(arr):
        return pl.BlockSpec(arr.shape, lambda b, s: (0,) * arr.ndim)

    cw = conv_w
    cb = conv_b.reshape(1, -1)
    return pl.pallas_call(
        functools.partial(_ffn_kernel, d_ff=d_ff, ff_chunk=ff_chunk, final_norm=final_norm),
        grid=(B, S // tm),
        in_specs=[
            row_spec,
            pl.BlockSpec((None, SUBLANES, D), lambda b, s: (b, jnp.maximum(s * halo_blocks - 1, 0), 0)),
            pl.BlockSpec((None, SUBLANES, D),
                         lambda b, s: (b, jnp.minimum((s + 1) * halo_blocks, n_halo - 1), 0)),
            pl.BlockSpec((None, 6, D), lambda b, s: (b, 0, 0)),
            pl.BlockSpec((1, D), lambda b, s: (0, 0)),
            full(w_up), full(cw), full(cb), full(w_down),
            pl.BlockSpec((1, D), lambda b, s: (0, 0)),
        ],
        out_specs=row_spec,
        out_shape=jax.ShapeDtypeStruct((B, S, D), F32),
        scratch_shapes=[
            pltpu.VMEM((tm + 2 * SUBLANES, ff_chunk), F32),
            pltpu.VMEM((tm, d_ff), BF16),
        ],
        compiler_params=_params(("arbitrary", "arbitrary")),
        name="convffn",
    )(x, x, x, mod, norm_w.reshape(1, D), w_up, cw, cb, w_down, final_w.reshape(1, D))


def _rope_tables(seq):
    half = HEAD_DIM // 2
    inv_freq = ROPE_THETA ** (-jnp.arange(0, half, dtype=F32) / half)
    ang = jnp.arange(seq, dtype=F32)[:, None] * inv_freq[None, :]
    cos, sin = jnp.cos(ang), jnp.sin(ang)
    reps = LANES // HEAD_DIM
    cos_t = jnp.tile(jnp.concatenate([cos, cos], axis=1), (1, reps))
    sin_t = jnp.tile(jnp.concatenate([-sin, sin], axis=1), (1, reps))
    return cos_t, sin_t


def _pick(size, pref):
    return pref if size % pref == 0 else size


def kernel(x, c, norm1_w, norm2_w, ada_w, ada_b, w_in, lam_q1, lam_k1, lam_q2, lam_k2, subln_w, w_attn_out, ssm_lam_re, ssm_lam_im, ssm_log_dt, ssm_b_re, ssm_b_im, ssm_c_re, ssm_c_im, ssm_d, w_glu_a, w_glu_b, w_out, w_up, conv_w, conv_b, w_down, final_w):
    B, S, D = x.shape
    depth = ada_w.shape[0]
    ssm_width = ssm_d.shape[1]
    assert B == SUBLANES, "the S5 scan maps the batch onto the 8 sublanes of a vreg"
    assert w_in.shape[2] == 3 * N_HEADS * HEAD_WIDTH + ssm_width + 2 * D
    tm = _pick(S, 512)
    tq = _pick(S, 512)
    tk = _pick(S, 1024)
    steps = _pick(S, 64)

    cos_t, sin_t = _rope_tables(S)
    mod_all = _adaln(c, ada_w, ada_b).reshape(depth, B, 6, D)

    for l in range(depth):
        lambda_init = 0.8 - 0.6 * math.exp(-0.3 * l)
        mod = mod_all[l]
        q, k, v, u_tm, ga, gs = _inproj(x, mod, norm1_w[l], w_in[l].astype(BF16), cos_t, sin_t, ssm_width, tm)

        lam = (jnp.exp(jnp.sum(lam_q1[l] * lam_k1[l])) - jnp.exp(jnp.sum(lam_q2[l] * lam_k2[l]))
               + lambda_init).astype(F32)
        o_attn = _attention(q, k, v, lam, subln_w[l], 1.0 - lambda_init, tq, tk)

        dirs = [_s5_discretize(ssm_lam_re[l, d], ssm_lam_im[l, d], ssm_log_dt[l, d], ssm_b_re[l, d],
                               ssm_b_im[l, d], ssm_c_re[l, d], ssm_c_im[l, d], B) for d in range(2)]
        yf, yb = _s5(u_tm.reshape(S * B, ssm_width), dirs[0], dirs[1], B, steps)

        w_glu = jnp.concatenate([w_glu_a[l], w_glu_b[l]], axis=1).astype(BF16)
        x = _merge(x, mod, o_attn, yf, yb, u_tm, ssm_d[l], ga, gs,
                   w_attn_out[l].astype(BF16), w_glu, w_out[l].astype(BF16), tm)
        x = _ffn(x, mod, norm2_w[l], w_up[l].astype(BF16), conv_w[l], conv_b[l], w_down[l].astype(BF16),
                 final_w, l == depth - 1, tm)
    return x
```

```python
import functools
import math

import jax
import jax.numpy as jnp
from jax import lax
from jax.experimental import pallas as pl
from jax.experimental.pallas import tpu as pltpu

N_HEADS = 8
HEAD_DIM = 64
HEAD_WIDTH = 2 * HEAD_DIM
ROPE_THETA = 10000.0
SSM_GROUP = 16
SSM_STATE = 64
CONV_WIDTH = 3
EPS = 1e-6
SUBLN_EPS = 1e-5
LOG2_E = math.log2(math.e)

LANES = 128
SUBLANES = 8
MXU_DIM = 256
VMEM_LIMIT_BYTES = 56 * 1024 * 1024

BF16 = jnp.bfloat16
F32 = jnp.float32


def _params(semantics):
    return pltpu.CompilerParams(dimension_semantics=semantics, vmem_limit_bytes=VMEM_LIMIT_BYTES)


def _modulated_rmsnorm(x, w, scale, shift):
    y = x * lax.rsqrt(jnp.mean(x * x, axis=-1, keepdims=True) + EPS)
    return (y * w) * (1.0 + scale) + shift


def _adaln_kernel(c_ref, w_ref, b_ref, o_ref):
    c = c_ref[...]
    c = c * jax.nn.sigmoid(c)
    w = w_ref[...]
    c_hi = c.astype(BF16)
    c_lo = (c - c_hi.astype(F32)).astype(BF16)
    w_hi = w.astype(BF16)
    w_lo = (w - w_hi.astype(F32)).astype(BF16)
    acc = jnp.dot(c_hi, w_hi, preferred_element_type=F32)
    acc += jnp.dot(c_hi, w_lo, preferred_element_type=F32)
    acc += jnp.dot(c_lo, w_hi, preferred_element_type=F32)
    o_ref[...] = acc + b_ref[...]


def _adaln(c, ada_w, ada_b):
    L, D, N = ada_w.shape
    B = c.shape[0]
    tn = 1536 if N % 1536 == 0 else N
    return pl.pallas_call(
        _adaln_kernel,
        grid=(L, N // tn),
        in_specs=[
            pl.BlockSpec((B, D), lambda l, j: (0, 0)),
            pl.BlockSpec((None, D, tn), lambda l, j: (l, 0, j)),
            pl.BlockSpec((None, 1, tn), lambda l, j: (l, 0, j)),
        ],
        out_specs=pl.BlockSpec((None, B, tn), lambda l, j: (l, 0, j)),
        out_shape=jax.ShapeDtypeStruct((L, B, N), F32),
        compiler_params=_params(("arbitrary", "arbitrary")),
        name="adaln",
    )(c, ada_w, ada_b.reshape(L, 1, N))


def _inproj_kernel(x_ref, mod_ref, nw_ref, w_ref, cos_ref, sin_ref,
                   q_ref, k_ref, v_ref, u_ref, ga_ref, gs_ref, *, d_model, ssm_width):
    x = x_ref[...]
    h = _modulated_rmsnorm(x, nw_ref[...], mod_ref[1:2, :], mod_ref[0:1, :]).astype(BF16)
    cos = cos_ref[...]
    sin = sin_ref[...]
    first_half = (lax.broadcasted_iota(jnp.int32, cos.shape, 1) % HEAD_DIM) < (HEAD_DIM // 2)
    qk_width = N_HEADS * HEAD_WIDTH

    def rope_heads(col0, out_ref, scale):
        for pair in range(N_HEADS // 2):
            c0 = col0 + pair * 2 * HEAD_WIDTH
            t2 = jnp.dot(h, w_ref[:, c0:c0 + 2 * HEAD_WIDTH], preferred_element_type=F32)
            for i in range(2):
                t = t2[:, i * HEAD_WIDTH:(i + 1) * HEAD_WIDTH]
                partner = jnp.where(first_half,
                                    pltpu.roll(t, HEAD_WIDTH - HEAD_DIM // 2, axis=1),
                                    pltpu.roll(t, HEAD_DIM // 2, axis=1))
                r = t * cos + partner * sin
                if scale != 1.0:
                    r = r * scale
                out_ref[2 * pair + i] = r.astype(BF16)

    rope_heads(0, q_ref, LOG2_E / math.sqrt(HEAD_DIM))
    rope_heads(qk_width, k_ref, 1.0)
    for pair in range(N_HEADS // 2):
        c0 = 2 * qk_width + pair * 2 * HEAD_WIDTH
        t2 = jnp.dot(h, w_ref[:, c0:c0 + 2 * HEAD_WIDTH], preferred_element_type=F32).astype(BF16)
        v_ref[2 * pair] = t2[:, :HEAD_WIDTH]
        v_ref[2 * pair + 1] = t2[:, HEAD_WIDTH:]
    c0 = 3 * qk_width
    u_ref[...] = jnp.dot(h, w_ref[:, c0:c0 + ssm_width], preferred_element_type=F32).astype(BF16)
    c0 += ssm_width
    ga_ref[...] = jax.nn.sigmoid(
        jnp.dot(h, w_ref[:, c0:c0 + d_model], preferred_element_type=F32)).astype(BF16)
    c0 += d_model
    gs_ref[...] = jax.nn.sigmoid(
        jnp.dot(h, w_ref[:, c0:c0 + d_model], preferred_element_type=F32)).astype(BF16)


def _inproj(x, mod, norm_w, w_in, cos, sin, ssm_width, tm):
    B, S, D = x.shape
    n_in = w_in.shape[1]
    head_shape = jax.ShapeDtypeStruct((B, N_HEADS, S, HEAD_WIDTH), BF16)
    head_spec = pl.BlockSpec((None, N_HEADS, tm, HEAD_WIDTH), lambda b, s: (b, 0, s, 0))
    row_spec = pl.BlockSpec((None, tm, D), lambda b, s: (b, s, 0))
    return pl.pallas_call(
        functools.partial(_inproj_kernel, d_model=D, ssm_width=ssm_width),
        grid=(B, S // tm),
        in_specs=[
            row_spec,
            pl.BlockSpec((None, 6, D), lambda b, s: (b, 0, 0)),
            pl.BlockSpec((1, D), lambda b, s: (0, 0)),
            pl.BlockSpec((D, n_in), lambda b, s: (0, 0)),
            pl.BlockSpec((tm, LANES), lambda b, s: (s, 0)),
            pl.BlockSpec((tm, LANES), lambda b, s: (s, 0)),
        ],
        out_specs=[
            head_spec, head_spec, head_spec,
            pl.BlockSpec((None, tm, ssm_width), lambda b, s: (b, s, 0)),
            row_spec, row_spec,
        ],
        out_shape=[
            head_shape, head_shape, head_shape,
            jax.ShapeDtypeStruct((B, S, ssm_width), BF16),
            jax.ShapeDtypeStruct((B, S, D), BF16),
            jax.ShapeDtypeStruct((B, S, D), BF16),
        ],
        compiler_params=_params(("arbitrary", "arbitrary")),
        name="inproj",
    )(x, mod, norm_w.reshape(1, D), w_in, cos, sin)


def _attn_kernel(lam_ref, q_ref, k_ref, v_ref, sw_ref, o_ref, vaug_ref, *, tk, out_scale):
    S = k_ref.shape[0]

    @pl.when(pl.program_id(2) == 0)
    def _():
        vaug_ref[:, :HEAD_WIDTH] = v_ref[...]
        vaug_ref[:, HEAD_WIDTH:] = jnp.ones((S, MXU_DIM - HEAD_WIDTH), BF16)

    q = q_ref[...]
    lane = lax.broadcasted_iota(jnp.int32, q.shape, 1)
    zero = jnp.zeros_like(q)
    subs = []
    for q_sub in (jnp.where(lane < HEAD_DIM, q, zero), jnp.where(lane >= HEAD_DIM, q, zero)):
        m = jnp.full((q.shape[0], 1), -jnp.inf, F32)
        acc = jnp.zeros((q.shape[0], MXU_DIM), F32)
        for c in range(S // tk):
            kc = k_ref[c * tk:(c + 1) * tk, :]
            s = lax.dot_general(q_sub, kc, (((1,), (1,)), ((), ())), preferred_element_type=F32)
            m_new = jnp.maximum(m, jnp.max(s, axis=1, keepdims=True))
            alpha = jnp.exp2(m - m_new)
            p = jnp.exp2(s - m_new).astype(BF16)
            acc = acc * alpha + jnp.dot(p, vaug_ref[c * tk:(c + 1) * tk, :], preferred_element_type=F32)
            m = m_new
        subs.append(acc[:, :HEAD_WIDTH] / acc[:, HEAD_WIDTH:])
    o = subs[0] - lam_ref[0] * subs[1]
    o = o * lax.rsqrt(jnp.mean(o * o, axis=-1, keepdims=True) + SUBLN_EPS)
    o_ref[...] = ((o * sw_ref[...]) * out_scale).astype(BF16)


def _attention(q, k, v, lam, sub_w, out_scale, tq, tk):
    B, H, S, W = q.shape
    kv_spec = pl.BlockSpec((None, None, S, W), lambda b, h, i: (b, h, 0, 0))
    return pl.pallas_call(
        functools.partial(_attn_kernel, tk=tk, out_scale=out_scale),
        grid=(B, H, S // tq),
        in_specs=[
            pl.BlockSpec(memory_space=pltpu.SMEM),
            pl.BlockSpec((None, None, tq, W), lambda b, h, i: (b, h, i, 0)),
            kv_spec, kv_spec,
            pl.BlockSpec((1, W), lambda b, h, i: (0, 0)),
        ],
        out_specs=pl.BlockSpec((None, tq, W), lambda b, h, i: (b, i, h)),
        out_shape=jax.ShapeDtypeStruct((B, S, H * W), BF16),
        scratch_shapes=[pltpu.VMEM((S, MXU_DIM), BF16)],
        compiler_params=_params(("arbitrary", "arbitrary", "arbitrary")),
        name="diff_attn",
    )(lam.reshape(1), q, k, v, sub_w.reshape(1, W))


def _s5_discretize(lam_re, lam_im, log_dt, b_re, b_im, c_re, c_im, batch):
    G, P = lam_re.shape
    N = b_re.shape[-1]
    dt = jnp.exp(log_dt)[:, None]
    mag = jnp.exp(lam_re * dt)
    ang = lam_im * dt
    abar_re = mag * jnp.cos(ang)
    abar_im = mag * jnp.sin(ang)
    den = lam_re * lam_re + lam_im * lam_im
    nr = abar_re - 1.0
    ni = abar_im
    f_re = (nr * lam_re + ni * lam_im) / den
    f_im = (ni * lam_re - nr * lam_im) / den
    bb_re = f_re[..., None] * b_re - f_im[..., None] * b_im
    bb_im = f_re[..., None] * b_im + f_im[..., None] * b_re
    gh = G // 2
    eye = jnp.eye(gh, dtype=F32)

    def in_block(bb):
        return jnp.einsum('gpn,gk->gnkp', bb, eye).reshape(gh * N, gh * P)

    def out_block(cc):
        return jnp.einsum('gnp,gk->gpkn', cc, eye).reshape(gh * P, gh * N)

    b_mat = jnp.stack([jnp.concatenate([in_block(bb_re[h * gh:(h + 1) * gh]),
                                        in_block(bb_im[h * gh:(h + 1) * gh])], axis=1) for h in range(2)])
    c_mat = jnp.stack([jnp.concatenate([out_block(c_re[h * gh:(h + 1) * gh]),
                                        out_block(-c_im[h * gh:(h + 1) * gh])], axis=0) for h in range(2)])
    a = jnp.stack([abar_re.reshape(G * P), abar_im.reshape(G * P)])
    a = jnp.broadcast_to(a[:, None, :], (2, batch, G * P))
    return b_mat.astype(BF16), c_mat.astype(BF16), a


def _s5_kernel(uf_ref, ub_ref, bf_ref, bb_ref, cf_ref, cb_ref, af_ref, ab_ref,
               yf_ref, yb_ref, tm_ref, xf_ref, xb_ref, carf_ref, carb_ref, *, steps, batch, lane_chunk):
    half_in = uf_ref.shape[2] // 2
    half_state = af_ref.shape[2] // 2

    @pl.when(pl.program_id(0) == 0)
    def _():
        carf_ref[...] = jnp.zeros_like(carf_ref)
        carb_ref[...] = jnp.zeros_like(carb_ref)

    slabs = half_in // LANES

    for u_ref, b_ref, x_ref in ((uf_ref, bf_ref, xf_ref), (ub_ref, bb_ref, xb_ref)):
        for b in range(batch):
            ub = u_ref[b].astype(F32)
            for j in range(2 * slabs):
                tm_ref[j, pl.ds(b, steps, stride=batch), :] = ub[:, j * LANES:(j + 1) * LANES]
        for h in range(2):
            cols = slice(2 * half_state * h, 2 * half_state * (h + 1))
            lhs = jnp.concatenate([tm_ref[h * slabs + j] for j in range(slabs)], axis=1).astype(BF16)
            x_ref[:, cols] = jnp.dot(lhs, b_ref[h], preferred_element_type=F32)

    for h in range(2):
        for j in range(half_state // lane_chunk):
            re = slice(2 * half_state * h + lane_chunk * j, 2 * half_state * h + lane_chunk * (j + 1))
            im = slice(re.start + half_state, re.stop + half_state)
            ac = slice(half_state * h + lane_chunk * j, half_state * h + lane_chunk * (j + 1))
            arf, aif = af_ref[0, :, ac], af_ref[1, :, ac]
            arb, aib = ab_ref[0, :, ac], ab_ref[1, :, ac]

            def body(t, carry):
                fr, fi, br, bi = carry
                rf = pl.ds(pl.multiple_of(t * batch, batch), batch)
                rb = pl.ds(pl.multiple_of((steps - 1 - t) * batch, batch), batch)
                nfr = arf * fr - aif * fi + xf_ref[rf, re]
                nfi = arf * fi + aif * fr + xf_ref[rf, im]
                nbr = arb * br - aib * bi + xb_ref[rb, re]
                nbi = arb * bi + aib * br + xb_ref[rb, im]
                xf_ref[rf, re] = nfr
                xf_ref[rf, im] = nfi
                xb_ref[rb, re] = nbr
                xb_ref[rb, im] = nbi
                return nfr, nfi, nbr, nbi

            fr, fi, br, bi = lax.fori_loop(
                0, steps, body, (carf_ref[:, re], carf_ref[:, im], carb_ref[:, re], carb_ref[:, im]),
                unroll=8)
            carf_ref[:, re] = fr
            carf_ref[:, im] = fi
            carb_ref[:, re] = br
            carb_ref[:, im] = bi

    for x_ref, c_ref, y_ref in ((xf_ref, cf_ref, yf_ref), (xb_ref, cb_ref, yb_ref)):
        for h in range(2):
            cols = slice(2 * half_state * h, 2 * half_state * (h + 1))
            y = jnp.dot(x_ref[:, cols].astype(BF16), c_ref[h], preferred_element_type=F32)
            for j in range(slabs):
                tm_ref[h * slabs + j] = y[:, j * LANES:(j + 1) * LANES]
        for b in range(batch):
            for j in range(2 * slabs):
                y_ref[b, :, j * LANES:(j + 1) * LANES] = tm_ref[j, pl.ds(b, steps, stride=batch), :]


def _s5(u, fwd, bwd, steps):
    batch, S, W = u.shape
    rows = steps * batch
    n = S // steps
    (bf, cf, af), (bb, cb, ab) = fwd, bwd
    n_state = af.shape[2]

    def full(arr):
        return pl.BlockSpec(arr.shape, lambda i: (0,) * arr.ndim)

    return pl.pallas_call(
        functools.partial(_s5_kernel, steps=steps, batch=batch, lane_chunk=512),
        grid=(n,),
        in_specs=[
            pl.BlockSpec((batch, steps, W), lambda i: (0, i, 0)),
            pl.BlockSpec((batch, steps, W), lambda i: (0, n - 1 - i, 0)),
            full(bf), full(bb), full(cf), full(cb), full(af), full(ab),
        ],
        out_specs=[
            pl.BlockSpec((batch, steps, W), lambda i: (0, i, 0)),
            pl.BlockSpec((batch, steps, W), lambda i: (0, n - 1 - i, 0)),
        ],
        out_shape=[jax.ShapeDtypeStruct((batch, S, W), F32)] * 2,
        scratch_shapes=[
            pltpu.VMEM((W // LANES, rows, LANES), F32),
            pltpu.VMEM((rows, 2 * n_state), F32), pltpu.VMEM((rows, 2 * n_state), F32),
            pltpu.VMEM((batch, 2 * n_state), F32), pltpu.VMEM((batch, 2 * n_state), F32),
        ],
        compiler_params=_params(("arbitrary",)),
        name="s5_scan",
    )(u, u, bf, bb, cf, cb, af, ab)


def _gelu_tanh(y):
    return 0.5 * y * (1.0 + jnp.tanh(math.sqrt(2.0 / math.pi) * (y + 0.044715 * (y * y * y))))


def _merge_kernel(x_ref, mod_ref, oa_ref, yf_ref, yb_ref, u_ref, d_ref, ga_ref, gs_ref,
                  wao_ref, wglu_ref, wout_ref, o_ref, *, d_model):
    y = yf_ref[...] + yb_ref[...] + d_ref[...] * u_ref[...].astype(F32)
    y = _gelu_tanh(y).astype(BF16)
    ab = jnp.dot(y, wglu_ref[...], preferred_element_type=F32)
    o_ssm = ab[:, :d_model] * jax.nn.sigmoid(ab[:, d_model:])
    o_attn = jnp.dot(oa_ref[...], wao_ref[...], preferred_element_type=F32)
    merged = ga_ref[...].astype(F32) * o_attn + gs_ref[...].astype(F32) * o_ssm
    out = jnp.dot(merged.astype(BF16), wout_ref[...], preferred_element_type=F32)
    o_ref[...] = x_ref[...] + mod_ref[2:3, :] * out


def _merge(x, mod, o_attn, yf, yb, u, ssm_d, ga, gs, w_attn_out, w_glu, w_out, tm):
    B, S, D = x.shape
    W = ssm_d.shape[0]
    row_spec = pl.BlockSpec((None, tm, D), lambda b, s: (b, s, 0))
    tm_spec = pl.BlockSpec((None, tm, W), lambda b, s: (b, s, 0))

    def full(arr):
        return pl.BlockSpec(arr.shape, lambda b, s: (0,) * arr.ndim)

    return pl.pallas_call(
        functools.partial(_merge_kernel, d_model=D),
        grid=(B, S // tm),
        in_specs=[
            row_spec,
            pl.BlockSpec((None, 6, D), lambda b, s: (b, 0, 0)),
            row_spec, tm_spec, tm_spec, tm_spec,
            pl.BlockSpec((1, W), lambda b, s: (0, 0)),
            row_spec, row_spec,
            full(w_attn_out), full(w_glu), full(w_out),
        ],
        out_specs=row_spec,
        out_shape=jax.ShapeDtypeStruct((B, S, D), F32),
        compiler_params=_params(("arbitrary", "arbitrary")),
        name="merge",
    )(x, mod, o_attn, yf, yb, u, ssm_d.reshape(1, W), ga, gs, w_attn_out, w_glu, w_out)


def _ffn_kernel(x_ref, xp_ref, xn_ref, mod_ref, nw_ref, wup_ref, cw_ref, cb_ref, wdn_ref, fw_ref,
                o_ref, hp_ref, up_ref, act_ref, *, d_ff, ff_chunk, final_norm):
    tm = x_ref.shape[0]
    seg = tm // SUBLANES
    body = slice(SUBLANES, SUBLANES + tm)
    si = pl.program_id(1)
    keep_prev = (si > 0).astype(F32)
    keep_next = (si < pl.num_programs(1) - 1).astype(F32)
    nw, scale, shift = nw_ref[...], mod_ref[4:5, :], mod_ref[3:4, :]
    slabs = hp_ref.shape[0]

    def put(rows, val):
        for j in range(slabs):
            hp_ref[j, rows, :] = val[:, j * LANES:(j + 1) * LANES]

    put(slice(0, SUBLANES), _modulated_rmsnorm(xp_ref[...], nw, scale, shift) * keep_prev)
    put(slice(SUBLANES + tm, 2 * SUBLANES + tm), _modulated_rmsnorm(xn_ref[...], nw, scale, shift) * keep_next)
    for r in range(SUBLANES):
        put(pl.ds(SUBLANES + r, seg, stride=SUBLANES),
            _modulated_rmsnorm(x_ref[r * seg:(r + 1) * seg, :], nw, scale, shift))
    h = jnp.concatenate([hp_ref[j] for j in range(slabs)], axis=1).astype(BF16)
    sub = lax.broadcasted_iota(jnp.int32, (SUBLANES, ff_chunk), 0)

    def conv(col0):
        up_ref[...] = jnp.dot(h, wup_ref[:, col0:col0 + ff_chunk], preferred_element_type=F32)
        up_ref[0:SUBLANES, :] = jnp.where(
            sub == 0, pltpu.roll(up_ref[0:SUBLANES, :], 1, axis=0),
            pltpu.roll(up_ref[tm:SUBLANES + tm, :], 1, axis=0))
        up_ref[SUBLANES + tm:, :] = jnp.where(
            sub == SUBLANES - 1, pltpu.roll(up_ref[SUBLANES + tm:, :], SUBLANES - 1, axis=0),
            pltpu.roll(up_ref[SUBLANES:2 * SUBLANES, :], SUBLANES - 1, axis=0))
        cw = cw_ref[:, col0:col0 + ff_chunk]
        return (up_ref[0:tm, :] * cw[0:1, :]
                + up_ref[body, :] * cw[1:2, :]
                + up_ref[2 * SUBLANES:2 * SUBLANES + tm, :] * cw[2:3, :]
                + cb_ref[:, col0:col0 + ff_chunk])

    for c in range(d_ff // ff_chunk):
        val = conv(c * ff_chunk)
        gt = conv(d_ff + c * ff_chunk)
        act_ref[:, c * ff_chunk:(c + 1) * ff_chunk] = ((gt * jax.nn.sigmoid(gt)) * val).astype(BF16)

    put(body, mod_ref[5:6, :] * jnp.dot(act_ref[...], wdn_ref[...], preferred_element_type=F32))
    for r in range(SUBLANES):
        rows = pl.ds(SUBLANES + r, seg, stride=SUBLANES)
        xo = x_ref[r * seg:(r + 1) * seg, :] + jnp.concatenate(
            [hp_ref[j, rows, :] for j in range(slabs)], axis=1)
        if final_norm:
            xo = (xo * lax.rsqrt(jnp.mean(xo * xo, axis=-1, keepdims=True) + EPS)) * fw_ref[...]
        o_ref[r * seg:(r + 1) * seg, :] = xo


def _ffn(x, mod, norm_w, w_up, conv_w, conv_b, w_down, final_w, final_norm, tm):
    B, S, D = x.shape
    d_ff = w_down.shape[0]
    ff_chunk = MXU_DIM if d_ff % MXU_DIM == 0 else d_ff
    halo_blocks = tm // SUBLANES
    n_halo = S // SUBLANES
    row_spec = pl.BlockSpec((None, tm, D), lambda b, s: (b, s, 0))

    def full(arr):
        return pl.BlockSpec(arr.shape, lambda b, s: (0,) * arr.ndim)

    cw = conv_w
    cb = conv_b.reshape(1, -1)
    return pl.pallas_call(
        functools.partial(_ffn_kernel, d_ff=d_ff, ff_chunk=ff_chunk, final_norm=final_norm),
        grid=(B, S // tm),
        in_specs=[
            row_spec,
            pl.BlockSpec((None, SUBLANES, D), lambda b, s: (b, jnp.maximum(s * halo_blocks - 1, 0), 0)),
            pl.BlockSpec((None, SUBLANES, D),
                         lambda b, s: (b, jnp.minimum((s + 1) * halo_blocks, n_halo - 1), 0)),
            pl.BlockSpec((None, 6, D), lambda b, s: (b, 0, 0)),
            pl.BlockSpec((1, D), lambda b, s: (0, 0)),
            full(w_up), full(cw), full(cb), full(w_down),
            pl.BlockSpec((1, D), lambda b, s: (0, 0)),
        ],
        out_specs=row_spec,
        out_shape=jax.ShapeDtypeStruct((B, S, D), F32),
        scratch_shapes=[
            pltpu.VMEM((D // LANES, tm + 2 * SUBLANES, LANES), F32),
            pltpu.VMEM((tm + 2 * SUBLANES, ff_chunk), F32),
            pltpu.VMEM((tm, d_ff), BF16),
        ],
        compiler_params=_params(("arbitrary", "arbitrary")),
        name="convffn",
    )(x, x, x, mod, norm_w.reshape(1, D), w_up, cw, cb, w_down, final_w.reshape(1, D))


def _rope_tables(seq):
    half = HEAD_DIM // 2
    inv_freq = ROPE_THETA ** (-jnp.arange(0, half, dtype=F32) / half)
    ang = jnp.arange(seq, dtype=F32)[:, None] * inv_freq[None, :]
    cos, sin = jnp.cos(ang), jnp.sin(ang)
    reps = LANES // HEAD_DIM
    cos_t = jnp.tile(jnp.concatenate([cos, cos], axis=1), (1, reps))
    sin_t = jnp.tile(jnp.concatenate([-sin, sin], axis=1), (1, reps))
    return cos_t, sin_t


def _pick(size, pref):
    return pref if size % pref == 0 else size


def kernel(x, c, norm1_w, norm2_w, ada_w, ada_b, w_in, lam_q1, lam_k1, lam_q2, lam_k2, subln_w, w_attn_out, ssm_lam_re, ssm_lam_im, ssm_log_dt, ssm_b_re, ssm_b_im, ssm_c_re, ssm_c_im, ssm_d, w_glu_a, w_glu_b, w_out, w_up, conv_w, conv_b, w_down, final_w):
    B, S, D = x.shape
    depth = ada_w.shape[0]
    ssm_width = ssm_d.shape[1]
    assert B == SUBLANES, "the S5 scan maps the batch onto the 8 sublanes of a vreg"
    assert w_in.shape[2] == 3 * N_HEADS * HEAD_WIDTH + ssm_width + 2 * D
    tm = _pick(S, 512)
    tq = _pick(S, 1024)
    tk = _pick(S, 512)
    steps = _pick(S, 64)

    cos_t, sin_t = _rope_tables(S)
    mod_all = _adaln(c, ada_w, ada_b).reshape(depth, B, 6, D)

    for l in range(depth):
        lambda_init = 0.8 - 0.6 * math.exp(-0.3 * l)
        mod = mod_all[l]
        q, k, v, u, ga, gs = _inproj(x, mod, norm1_w[l], w_in[l].astype(BF16), cos_t, sin_t, ssm_width, tm)

        lam = (jnp.exp(jnp.sum(lam_q1[l] * lam_k1[l])) - jnp.exp(jnp.sum(lam_q2[l] * lam_k2[l]))
               + lambda_init).astype(F32)
        o_attn = _attention(q, k, v, lam, subln_w[l], 1.0 - lambda_init, tq, tk)

        dirs = [_s5_discretize(ssm_lam_re[l, d], ssm_lam_im[l, d], ssm_log_dt[l, d], ssm_b_re[l, d],
                               ssm_b_im[l, d], ssm_c_re[l, d], ssm_c_im[l, d], B) for d in range(2)]
        yf, yb = _s5(u, dirs[0], dirs[1], steps)

        w_glu = jnp.concatenate([w_glu_a[l], w_glu_b[l]], axis=1).astype(BF16)
        x = _merge(x, mod, o_attn, yf, yb, u, ssm_d[l], ga, gs,
                   w_attn_out[l].astype(BF16), w_glu, w_out[l].astype(BF16), tm)
        x = _ffn(x, mod, norm2_w[l], w_up[l].astype(BF16), conv_w[l], conv_b[l], w_down[l].astype(BF16),
                 final_w, l == depth - 1, tm)
    return x
```

```python
import functools
import math

import jax
import jax.numpy as jnp
from jax import lax
from jax.experimental import pallas as pl
from jax.experimental.pallas import tpu as pltpu

N_HEADS = 8
HEAD_DIM = 64
HEAD_WIDTH = 2 * HEAD_DIM
ROPE_THETA = 10000.0
SSM_GROUP = 16
SSM_STATE = 64
CONV_WIDTH = 3
EPS = 1e-6
SUBLN_EPS = 1e-5
LOG2_E = math.log2(math.e)

LANES = 128
SUBLANES = 8
MXU_DIM = 256
VMEM_LIMIT_BYTES = 56 * 1024 * 1024

BF16 = jnp.bfloat16
F32 = jnp.float32


def _params(semantics):
    return pltpu.CompilerParams(dimension_semantics=semantics, vmem_limit_bytes=VMEM_LIMIT_BYTES)


def _modulated_rmsnorm(x, w, scale, shift):
    y = x * lax.rsqrt(jnp.mean(x * x, axis=-1, keepdims=True) + EPS)
    return (y * w) * (1.0 + scale) + shift


def _adaln_kernel(c_ref, w_ref, b_ref, o_ref):
    c = c_ref[...]
    c = c * jax.nn.sigmoid(c)
    w = w_ref[...]
    c_hi = c.astype(BF16)
    c_lo = (c - c_hi.astype(F32)).astype(BF16)
    w_hi = w.astype(BF16)
    w_lo = (w - w_hi.astype(F32)).astype(BF16)
    acc = jnp.dot(c_hi, w_hi, preferred_element_type=F32)
    acc += jnp.dot(c_hi, w_lo, preferred_element_type=F32)
    acc += jnp.dot(c_lo, w_hi, preferred_element_type=F32)
    o_ref[...] = acc + b_ref[...]


def _adaln(c, ada_w, ada_b):
    L, D, N = ada_w.shape
    B = c.shape[0]
    tn = 1536 if N % 1536 == 0 else N
    return pl.pallas_call(
        _adaln_kernel,
        grid=(L, N // tn),
        in_specs=[
            pl.BlockSpec((B, D), lambda l, j: (0, 0)),
            pl.BlockSpec((None, D, tn), lambda l, j: (l, 0, j)),
            pl.BlockSpec((None, 1, tn), lambda l, j: (l, 0, j)),
        ],
        out_specs=pl.BlockSpec((None, B, tn), lambda l, j: (l, 0, j)),
        out_shape=jax.ShapeDtypeStruct((L, B, N), F32),
        compiler_params=_params(("arbitrary", "arbitrary")),
        name="adaln",
    )(c, ada_w, ada_b.reshape(L, 1, N))


def _inproj_kernel(x_ref, mod_ref, nw_ref, w_ref, cos_ref, sin_ref,
                   q_ref, k_ref, v_ref, u_ref, ga_ref, gs_ref, *, tm, d_model, ssm_width):
    qk_width = N_HEADS * HEAD_WIDTH
    first_half = (lax.broadcasted_iota(jnp.int32, (tm, LANES), 1) % HEAD_DIM) < (HEAD_DIM // 2)

    for st in range(x_ref.shape[0] // tm):
        rows = slice(st * tm, (st + 1) * tm)
        h = _modulated_rmsnorm(x_ref[rows, :], nw_ref[...], mod_ref[1:2, :], mod_ref[0:1, :]).astype(BF16)
        cos = cos_ref[rows, :]
        sin = sin_ref[rows, :]

        def rope_heads(col0, out_ref, scale):
            for pair in range(N_HEADS // 2):
                c0 = col0 + pair * 2 * HEAD_WIDTH
                t2 = jnp.dot(h, w_ref[:, c0:c0 + 2 * HEAD_WIDTH], preferred_element_type=F32)
                for i in range(2):
                    t = t2[:, i * HEAD_WIDTH:(i + 1) * HEAD_WIDTH]
                    partner = jnp.where(first_half,
                                        pltpu.roll(t, HEAD_WIDTH - HEAD_DIM // 2, axis=1),
                                        pltpu.roll(t, HEAD_DIM // 2, axis=1))
                    r = t * cos + partner * sin
                    if scale != 1.0:
                        r = r * scale
                    out_ref[2 * pair + i, rows, :] = r.astype(BF16)

        rope_heads(0, q_ref, LOG2_E / math.sqrt(HEAD_DIM))
        rope_heads(qk_width, k_ref, 1.0)
        for pair in range(N_HEADS // 2):
            c0 = 2 * qk_width + pair * 2 * HEAD_WIDTH
            t2 = jnp.dot(h, w_ref[:, c0:c0 + 2 * HEAD_WIDTH], preferred_element_type=F32).astype(BF16)
            v_ref[2 * pair, rows, :] = t2[:, :HEAD_WIDTH]
            v_ref[2 * pair + 1, rows, :] = t2[:, HEAD_WIDTH:]
        c0 = 3 * qk_width
        u_ref[rows, :] = jnp.dot(h, w_ref[:, c0:c0 + ssm_width], preferred_element_type=F32).astype(BF16)
        c0 += ssm_width
        ga_ref[rows, :] = jax.nn.sigmoid(
            jnp.dot(h, w_ref[:, c0:c0 + d_model], preferred_element_type=F32)).astype(BF16)
        c0 += d_model
        gs_ref[rows, :] = jax.nn.sigmoid(
            jnp.dot(h, w_ref[:, c0:c0 + d_model], preferred_element_type=F32)).astype(BF16)


def _resident(arr, *lead):
    shape = (None,) * len(lead) + arr.shape[len(lead):]
    index = tuple(lead) + (0,) * (arr.ndim - len(lead))
    return pl.BlockSpec(shape, lambda *_: index, pipeline_mode=pl.Buffered(1))


def _inproj(x, mod, norm_w, w_in, layer, cos, sin, ssm_width, tm, tiles_per_step):
    B, S, D = x.shape
    rows = tm * tiles_per_step
    head_shape = jax.ShapeDtypeStruct((B, N_HEADS, S, HEAD_WIDTH), BF16)
    head_spec = pl.BlockSpec((None, N_HEADS, rows, HEAD_WIDTH), lambda b, s: (b, 0, s, 0))
    row_spec = pl.BlockSpec((None, rows, D), lambda b, s: (b, s, 0))
    return pl.pallas_call(
        functools.partial(_inproj_kernel, tm=tm, d_model=D, ssm_width=ssm_width),
        grid=(B, S // rows),
        in_specs=[
            row_spec,
            pl.BlockSpec((None, 6, D), lambda b, s: (b, 0, 0)),
            _resident(norm_w, layer),
            _resident(w_in, layer),
            pl.BlockSpec((rows, LANES), lambda b, s: (s, 0)),
            pl.BlockSpec((rows, LANES), lambda b, s: (s, 0)),
        ],
        out_specs=[
            head_spec, head_spec, head_spec,
            pl.BlockSpec((None, rows, ssm_width), lambda b, s: (b, s, 0)),
            row_spec, row_spec,
        ],
        out_shape=[
            head_shape, head_shape, head_shape,
            jax.ShapeDtypeStruct((B, S, ssm_width), BF16),
            jax.ShapeDtypeStruct((B, S, D), BF16),
            jax.ShapeDtypeStruct((B, S, D), BF16),
        ],
        compiler_params=_params(("arbitrary", "arbitrary")),
        name="inproj",
    )(x, mod, norm_w, w_in, cos, sin)


def _attn_kernel(lam_ref, q_ref, k_ref, v_ref, sw_ref, o_ref, vaug_ref, *, tq, tk, out_scale):
    S = k_ref.shape[0]

    @pl.when(pl.program_id(2) == 0)
    def _():
        vaug_ref[:, :HEAD_WIDTH] = v_ref[...]
        vaug_ref[:, HEAD_WIDTH:] = jnp.ones((S, MXU_DIM - HEAD_WIDTH), BF16)

    for qt in range(q_ref.shape[0] // tq):
        rows = slice(qt * tq, (qt + 1) * tq)
        q = q_ref[rows, :]
        lane = lax.broadcasted_iota(jnp.int32, q.shape, 1)
        zero = jnp.zeros_like(q)
        subs = []
        for q_sub in (jnp.where(lane < HEAD_DIM, q, zero), jnp.where(lane >= HEAD_DIM, q, zero)):
            m = jnp.full((tq, 1), -jnp.inf, F32)
            acc = jnp.zeros((tq, MXU_DIM), F32)
            for c in range(S // tk):
                kc = k_ref[c * tk:(c + 1) * tk, :]
                s = lax.dot_general(q_sub, kc, (((1,), (1,)), ((), ())), preferred_element_type=F32)
                m_new = jnp.maximum(m, jnp.max(s, axis=1, keepdims=True))
                alpha = jnp.exp2(m - m_new)
                p = jnp.exp2(s - m_new).astype(BF16)
                acc = acc * alpha + jnp.dot(p, vaug_ref[c * tk:(c + 1) * tk, :], preferred_element_type=F32)
                m = m_new
            subs.append(acc[:, :HEAD_WIDTH] / acc[:, HEAD_WIDTH:])
        o = subs[0] - lam_ref[0] * subs[1]
        o = o * lax.rsqrt(jnp.mean(o * o, axis=-1, keepdims=True) + SUBLN_EPS)
        o_ref[rows, :] = ((o * sw_ref[...]) * out_scale).astype(BF16)


def _attention(q, k, v, lam, sub_w, layer, out_scale, tq, tk, tiles_per_step):
    B, H, S, W = q.shape
    rows = tq * tiles_per_step
    kv_spec = pl.BlockSpec((None, None, S, W), lambda b, h, i: (b, h, 0, 0))
    return pl.pallas_call(
        functools.partial(_attn_kernel, tq=tq, tk=tk, out_scale=out_scale),
        grid=(B, H, S // rows),
        in_specs=[
            pl.BlockSpec(memory_space=pltpu.SMEM),
            pl.BlockSpec((None, None, rows, W), lambda b, h, i: (b, h, i, 0)),
            kv_spec, kv_spec,
            _resident(sub_w, layer),
        ],
        out_specs=pl.BlockSpec((None, rows, W), lambda b, h, i: (b, i, h)),
        out_shape=jax.ShapeDtypeStruct((B, S, H * W), BF16),
        scratch_shapes=[pltpu.VMEM((S, MXU_DIM), BF16)],
        compiler_params=_params(("arbitrary", "arbitrary", "arbitrary")),
        name="diff_attn",
    )(lam.reshape(1), q, k, v, sub_w)


def _s5_discretize(lam_re, lam_im, log_dt, b_re, b_im, c_re, c_im, batch):
    lead = lam_re.shape[:-2]
    G, P = lam_re.shape[-2:]
    N = b_re.shape[-1]
    dt = jnp.exp(log_dt)[..., None]
    mag = jnp.exp(lam_re * dt)
    ang = lam_im * dt
    abar_re = mag * jnp.cos(ang)
    abar_im = mag * jnp.sin(ang)
    den = lam_re * lam_re + lam_im * lam_im
    nr = abar_re - 1.0
    ni = abar_im
    f_re = (nr * lam_re + ni * lam_im) / den
    f_im = (ni * lam_re - nr * lam_im) / den
    bb_re = f_re[..., None] * b_re - f_im[..., None] * b_im
    bb_im = f_re[..., None] * b_im + f_im[..., None] * b_re
    gh = G // 2
    eye = jnp.eye(gh, dtype=F32)

    def in_block(bb):
        bb = bb.reshape(lead + (2, gh, P, N))
        return jnp.einsum('...gpn,gk->...gnkp', bb, eye).reshape(lead + (2, gh * N, gh * P))

    def out_block(cc):
        cc = cc.reshape(lead + (2, gh, N, P))
        return jnp.einsum('...gnp,gk->...gpkn', cc, eye).reshape(lead + (2, gh * P, gh * N))

    b_mat = jnp.concatenate([in_block(bb_re), in_block(bb_im)], axis=-1)
    c_mat = jnp.concatenate([out_block(c_re), out_block(-c_im)], axis=-2)
    a = jnp.stack([abar_re.reshape(lead + (G * P,)), abar_im.reshape(lead + (G * P,))], axis=-2)
    a = jnp.broadcast_to(a[..., None, :], lead + (2, batch, G * P))
    return b_mat.astype(BF16), c_mat.astype(BF16), a


def _s5_kernel(uf_ref, ub_ref, bf_ref, bb_ref, cf_ref, cb_ref, af_ref, ab_ref,
               yf_ref, yb_ref, tm_ref, xf_ref, xb_ref, carf_ref, carb_ref, *, steps, batch, lane_chunk):
    half_in = uf_ref.shape[2] // 2
    half_state = af_ref.shape[2] // 2

    @pl.when(pl.program_id(0) == 0)
    def _():
        carf_ref[...] = jnp.zeros_like(carf_ref)
        carb_ref[...] = jnp.zeros_like(carb_ref)

    slabs = half_in // LANES

    for u_ref, b_ref, x_ref in ((uf_ref, bf_ref, xf_ref), (ub_ref, bb_ref, xb_ref)):
        for b in range(batch):
            ub = u_ref[b].astype(F32)
            for j in range(2 * slabs):
                tm_ref[j, pl.ds(b, steps, stride=batch), :] = ub[:, j * LANES:(j + 1) * LANES]
        for h in range(2):
            cols = slice(2 * half_state * h, 2 * half_state * (h + 1))
            lhs = jnp.concatenate([tm_ref[h * slabs + j] for j in range(slabs)], axis=1).astype(BF16)
            x_ref[:, cols] = jnp.dot(lhs, b_ref[h], preferred_element_type=F32)

    for h in range(2):
        for j in range(half_state // lane_chunk):
            re = slice(2 * half_state * h + lane_chunk * j, 2 * half_state * h + lane_chunk * (j + 1))
            im = slice(re.start + half_state, re.stop + half_state)
            ac = slice(half_state * h + lane_chunk * j, half_state * h + lane_chunk * (j + 1))
            arf, aif = af_ref[0, :, ac], af_ref[1, :, ac]
            arb, aib = ab_ref[0, :, ac], ab_ref[1, :, ac]

            fr, fi, br, bi = carf_ref[:, re], carf_ref[:, im], carb_ref[:, re], carb_ref[:, im]
            for t in range(steps):
                rf = slice(t * batch, (t + 1) * batch)
                rb = slice((steps - 1 - t) * batch, (steps - t) * batch)
                fr, fi = arf * fr - aif * fi + xf_ref[rf, re], arf * fi + aif * fr + xf_ref[rf, im]
                br, bi = arb * br - aib * bi + xb_ref[rb, re], arb * bi + aib * br + xb_ref[rb, im]
                xf_ref[rf, re] = fr
                xf_ref[rf, im] = fi
                xb_ref[rb, re] = br
                xb_ref[rb, im] = bi
            carf_ref[:, re] = fr
            carf_ref[:, im] = fi
            carb_ref[:, re] = br
            carb_ref[:, im] = bi

    for x_ref, c_ref, y_ref in ((xf_ref, cf_ref, yf_ref), (xb_ref, cb_ref, yb_ref)):
        for h in range(2):
            cols = slice(2 * half_state * h, 2 * half_state * (h + 1))
            y = jnp.dot(x_ref[:, cols].astype(BF16), c_ref[h], preferred_element_type=F32)
            for j in range(slabs):
                tm_ref[h * slabs + j] = y[:, j * LANES:(j + 1) * LANES]
        for b in range(batch):
            for j in range(2 * slabs):
                y_ref[b, :, j * LANES:(j + 1) * LANES] = tm_ref[j, pl.ds(b, steps, stride=batch), :]


def _s5(u, b_mat, c_mat, a, layer, steps):
    batch, S, W = u.shape
    rows = steps * batch
    n = S // steps
    n_state = a.shape[-1]
    return pl.pallas_call(
        functools.partial(_s5_kernel, steps=steps, batch=batch, lane_chunk=512),
        grid=(n,),
        in_specs=[
            pl.BlockSpec((batch, steps, W), lambda i: (0, i, 0)),
            pl.BlockSpec((batch, steps, W), lambda i: (0, n - 1 - i, 0)),
            _resident(b_mat, layer, 0), _resident(b_mat, layer, 1),
            _resident(c_mat, layer, 0), _resident(c_mat, layer, 1),
            _resident(a, layer, 0), _resident(a, layer, 1),
        ],
        out_specs=[
            pl.BlockSpec((batch, steps, W), lambda i: (0, i, 0)),
            pl.BlockSpec((batch, steps, W), lambda i: (0, n - 1 - i, 0)),
        ],
        out_shape=[jax.ShapeDtypeStruct((batch, S, W), F32)] * 2,
        scratch_shapes=[
            pltpu.VMEM((W // LANES, rows, LANES), F32),
            pltpu.VMEM((rows, 2 * n_state), F32), pltpu.VMEM((rows, 2 * n_state), F32),
            pltpu.VMEM((batch, 2 * n_state), F32), pltpu.VMEM((batch, 2 * n_state), F32),
        ],
        compiler_params=_params(("arbitrary",)),
        name="s5_scan",
    )(u, u, b_mat, b_mat, c_mat, c_mat, a, a)


def _gelu_tanh(y):
    return 0.5 * y * (1.0 + jnp.tanh(math.sqrt(2.0 / math.pi) * (y + 0.044715 * (y * y * y))))


def _merge_kernel(x_ref, mod_ref, oa_ref, yf_ref, yb_ref, u_ref, d_ref, ga_ref, gs_ref,
                  wao_ref, wglu_ref, wout_ref, o_ref, *, tm, d_model):
    for st in range(x_ref.shape[0] // tm):
        rows = slice(st * tm, (st + 1) * tm)
        y = yf_ref[rows, :] + yb_ref[rows, :] + d_ref[...] * u_ref[rows, :].astype(F32)
        y = _gelu_tanh(y).astype(BF16)
        ab = jnp.dot(y, wglu_ref[...], preferred_element_type=F32)
        o_ssm = ab[:, :d_model] * jax.nn.sigmoid(ab[:, d_model:])
        o_attn = jnp.dot(oa_ref[rows, :], wao_ref[...], preferred_element_type=F32)
        merged = ga_ref[rows, :].astype(F32) * o_attn + gs_ref[rows, :].astype(F32) * o_ssm
        out = jnp.dot(merged.astype(BF16), wout_ref[...], preferred_element_type=F32)
        o_ref[rows, :] = x_ref[rows, :] + mod_ref[2:3, :] * out


def _merge(x, mod, o_attn, yf, yb, u, ssm_d, ga, gs, w_attn_out, w_glu, w_out, layer, tm, tiles_per_step):
    B, S, D = x.shape
    W = ssm_d.shape[-1]
    rows = tm * tiles_per_step
    row_spec = pl.BlockSpec((None, rows, D), lambda b, s: (b, s, 0))
    ssm_spec = pl.BlockSpec((None, rows, W), lambda b, s: (b, s, 0))
    return pl.pallas_call(
        functools.partial(_merge_kernel, tm=tm, d_model=D),
        grid=(B, S // rows),
        in_specs=[
            row_spec,
            pl.BlockSpec((None, 6, D), lambda b, s: (b, 0, 0)),
            row_spec, ssm_spec, ssm_spec, ssm_spec,
            _resident(ssm_d, layer),
            row_spec, row_spec,
            _resident(w_attn_out, layer), _resident(w_glu, layer), _resident(w_out, layer),
        ],
        out_specs=row_spec,
        out_shape=jax.ShapeDtypeStruct((B, S, D), F32),
        compiler_params=_params(("arbitrary", "arbitrary")),
        name="merge",
    )(x, mod, o_attn, yf, yb, u, ssm_d, ga, gs, w_attn_out, w_glu, w_out)


def _ffn_kernel(x_ref, xp_ref, xn_ref, mod_ref, nw_ref, wup_ref, cw_ref, cb_ref, wdn_ref, fw_ref,
                o_ref, hp_ref, up_ref, act_ref, *, tm, d_ff, ff_chunk, final_norm):
    n_sub = x_ref.shape[0] // tm
    seg = tm // SUBLANES
    body = slice(SUBLANES, SUBLANES + tm)
    si = pl.program_id(1)
    keep_prev = (si > 0).astype(F32)
    keep_next = (si < pl.num_programs(1) - 1).astype(F32)
    nw, scale, shift = nw_ref[...], mod_ref[4:5, :], mod_ref[3:4, :]
    slabs = hp_ref.shape[1]
    sub = lax.broadcasted_iota(jnp.int32, (SUBLANES, ff_chunk), 0)

    for st in range(n_sub):
        t0 = st * tm

        def put(rows, val):
            for j in range(slabs):
                hp_ref[st, j, rows, :] = val[:, j * LANES:(j + 1) * LANES]

        if st == 0:
            h_prev = _modulated_rmsnorm(xp_ref[...], nw, scale, shift) * keep_prev
        else:
            h_prev = _modulated_rmsnorm(x_ref[t0 - SUBLANES:t0, :], nw, scale, shift)
        if st == n_sub - 1:
            h_next = _modulated_rmsnorm(xn_ref[...], nw, scale, shift) * keep_next
        else:
            h_next = _modulated_rmsnorm(x_ref[t0 + tm:t0 + tm + SUBLANES, :], nw, scale, shift)
        put(slice(0, SUBLANES), h_prev)
        put(slice(SUBLANES + tm, 2 * SUBLANES + tm), h_next)
        for r in range(SUBLANES):
            put(pl.ds(SUBLANES + r, seg, stride=SUBLANES),
                _modulated_rmsnorm(x_ref[t0 + r * seg:t0 + (r + 1) * seg, :], nw, scale, shift))
        h = jnp.concatenate([hp_ref[st, j] for j in range(slabs)], axis=1).astype(BF16)

        def conv(col0):
            up_ref[...] = jnp.dot(h, wup_ref[:, col0:col0 + ff_chunk], preferred_element_type=F32)
            up_ref[0:SUBLANES, :] = jnp.where(
                sub == 0, pltpu.roll(up_ref[0:SUBLANES, :], 1, axis=0),
                pltpu.roll(up_ref[tm:SUBLANES + tm, :], 1, axis=0))
            up_ref[SUBLANES + tm:, :] = jnp.where(
                sub == SUBLANES - 1, pltpu.roll(up_ref[SUBLANES + tm:, :], SUBLANES - 1, axis=0),
                pltpu.roll(up_ref[SUBLANES:2 * SUBLANES, :], SUBLANES - 1, axis=0))
            cw = cw_ref[:, col0:col0 + ff_chunk]
            return (up_ref[0:tm, :] * cw[0:1, :]
                    + up_ref[body, :] * cw[1:2, :]
                    + up_ref[2 * SUBLANES:2 * SUBLANES + tm, :] * cw[2:3, :]
                    + cb_ref[:, col0:col0 + ff_chunk])

        for c in range(d_ff // ff_chunk):
            val = conv(c * ff_chunk)
            gt = conv(d_ff + c * ff_chunk)
            act_ref[st, :, c * ff_chunk:(c + 1) * ff_chunk] = ((gt * jax.nn.sigmoid(gt)) * val).astype(BF16)

        put(body, mod_ref[5:6, :] * jnp.dot(act_ref[st], wdn_ref[...], preferred_element_type=F32))
        for r in range(SUBLANES):
            rows = pl.ds(SUBLANES + r, seg, stride=SUBLANES)
            xo = x_ref[t0 + r * seg:t0 + (r + 1) * seg, :] + jnp.concatenate(
                [hp_ref[st, j, rows, :] for j in range(slabs)], axis=1)
            if final_norm:
                xo = (xo * lax.rsqrt(jnp.mean(xo * xo, axis=-1, keepdims=True) + EPS)) * fw_ref[...]
            o_ref[t0 + r * seg:t0 + (r + 1) * seg, :] = xo


def _ffn(x, mod, norm_w, w_up, conv_w, conv_b, w_down, final_w, layer, final_norm, tm, tiles_per_step):
    B, S, D = x.shape
    d_ff = w_down.shape[-2]
    ff_chunk = MXU_DIM if d_ff % MXU_DIM == 0 else d_ff
    rows = tm * tiles_per_step
    halo_blocks = rows // SUBLANES
    n_halo = S // SUBLANES
    row_spec = pl.BlockSpec((None, rows, D), lambda b, s: (b, s, 0))
    final_w = final_w.reshape(1, D)
    return pl.pallas_call(
        functools.partial(_ffn_kernel, tm=tm, d_ff=d_ff, ff_chunk=ff_chunk, final_norm=final_norm),
        grid=(B, S // rows),
        in_specs=[
            row_spec,
            pl.BlockSpec((None, SUBLANES, D), lambda b, s: (b, jnp.maximum(s * halo_blocks - 1, 0), 0)),
            pl.BlockSpec((None, SUBLANES, D),
                         lambda b, s: (b, jnp.minimum((s + 1) * halo_blocks, n_halo - 1), 0)),
            pl.BlockSpec((None, 6, D), lambda b, s: (b, 0, 0)),
            _resident(norm_w, layer),
            _resident(w_up, layer), _resident(conv_w, layer), _resident(conv_b, layer), _resident(w_down, layer),
            _resident(final_w),
        ],
        out_specs=row_spec,
        out_shape=jax.ShapeDtypeStruct((B, S, D), F32),
        scratch_shapes=[
            pltpu.VMEM((tiles_per_step, D // LANES, tm + 2 * SUBLANES, LANES), F32),
            pltpu.VMEM((tm + 2 * SUBLANES, ff_chunk), F32),
            pltpu.VMEM((tiles_per_step, tm, d_ff), BF16),
        ],
        compiler_params=_params(("arbitrary", "arbitrary")),
        name="convffn",
    )(x, x, x, mod, norm_w, w_up, conv_w, conv_b, w_down, final_w)


def _rope_tables(seq):
    half = HEAD_DIM // 2
    inv_freq = ROPE_THETA ** (-jnp.arange(0, half, dtype=F32) / half)
    ang = jnp.arange(seq, dtype=F32)[:, None] * inv_freq[None, :]
    cos, sin = jnp.cos(ang), jnp.sin(ang)
    reps = LANES // HEAD_DIM
    cos_t = jnp.tile(jnp.concatenate([cos, cos], axis=1), (1, reps))
    sin_t = jnp.tile(jnp.concatenate([-sin, sin], axis=1), (1, reps))
    return cos_t, sin_t


def _pick(size, pref):
    return pref if size % pref == 0 else size


def kernel(x, c, norm1_w, norm2_w, ada_w, ada_b, w_in, lam_q1, lam_k1, lam_q2, lam_k2, subln_w, w_attn_out, ssm_lam_re, ssm_lam_im, ssm_log_dt, ssm_b_re, ssm_b_im, ssm_c_re, ssm_c_im, ssm_d, w_glu_a, w_glu_b, w_out, w_up, conv_w, conv_b, w_down, final_w):
    B, S, D = x.shape
    depth = ada_w.shape[0]
    ssm_width = ssm_d.shape[1]
    assert B == SUBLANES, "the S5 scan maps the batch onto the 8 sublanes of a vreg"
    assert w_in.shape[2] == 3 * N_HEADS * HEAD_WIDTH + ssm_width + 2 * D
    tm = _pick(S, 512)
    tq = _pick(S, 1024)
    tk = _pick(S, 512)
    steps = _pick(S, 64)
    row_tiles = 2 if S % (2 * tm) == 0 else 1
    q_tiles = 2 if S % (2 * tq) == 0 else 1

    cos_t, sin_t = _rope_tables(S)
    mod_all = _adaln(c, ada_w, ada_b).reshape(depth, B, 6, D)
    w_in_b, w_up_b, w_down_b = w_in.astype(BF16), w_up.astype(BF16), w_down.astype(BF16)
    w_attn_out_b, w_out_b = w_attn_out.astype(BF16), w_out.astype(BF16)
    w_glu_b16 = jnp.concatenate([w_glu_a, w_glu_b], axis=2).astype(BF16)
    norm1, norm2 = norm1_w.reshape(depth, 1, D), norm2_w.reshape(depth, 1, D)
    sub_w = subln_w.reshape(depth, 1, HEAD_WIDTH)
    d_skip = ssm_d.reshape(depth, 1, ssm_width)
    conv_b3 = conv_b.reshape(depth, 1, -1)
    b_mat, c_mat, a_bar = _s5_discretize(ssm_lam_re, ssm_lam_im, ssm_log_dt, ssm_b_re, ssm_b_im,
                                         ssm_c_re, ssm_c_im, B)
    lam_all = (jnp.exp(jnp.sum(lam_q1 * lam_k1, axis=-1)) - jnp.exp(jnp.sum(lam_q2 * lam_k2, axis=-1))).astype(F32)

    for l in range(depth):
        lambda_init = 0.8 - 0.6 * math.exp(-0.3 * l)
        mod = mod_all[l]
        q, k, v, u, ga, gs = _inproj(x, mod, norm1, w_in_b, l, cos_t, sin_t, ssm_width, tm, row_tiles)
        o_attn = _attention(q, k, v, lam_all[l] + lambda_init, sub_w, l, 1.0 - lambda_init, tq, tk, q_tiles)
        yf, yb = _s5(u, b_mat, c_mat, a_bar, l, steps)
        x = _merge(x, mod, o_attn, yf, yb, u, d_skip, ga, gs, w_attn_out_b, w_glu_b16, w_out_b, l, tm, row_tiles)
        x = _ffn(x, mod, norm2, w_up_b, conv_w, conv_b3, w_down_b, final_w, l, l == depth - 1, tm, row_tiles)
    return x
```

```python
import functools
import math

import jax
import jax.numpy as jnp
from jax import lax
from jax.experimental import pallas as pl
from jax.experimental.pallas import tpu as pltpu

N_HEADS = 8
HEAD_DIM = 64
HEAD_WIDTH = 2 * HEAD_DIM
ROPE_THETA = 10000.0
SSM_GROUP = 16
SSM_STATE = 64
CONV_WIDTH = 3
EPS = 1e-6
SUBLN_EPS = 1e-5
LOG2_E = math.log2(math.e)

LANES = 128
SUBLANES = 8
MXU_DIM = 256
VMEM_LIMIT_BYTES = 56 * 1024 * 1024

BF16 = jnp.bfloat16
F32 = jnp.float32


def _params(semantics):
    return pltpu.CompilerParams(dimension_semantics=semantics, vmem_limit_bytes=VMEM_LIMIT_BYTES)


def _modulated_rmsnorm(x, w, scale, shift):
    y = x * lax.rsqrt(jnp.mean(x * x, axis=-1, keepdims=True) + EPS)
    return (y * w) * (1.0 + scale) + shift


def _adaln_kernel(c_ref, w_ref, b_ref, o_ref):
    c = c_ref[...]
    c = c * jax.nn.sigmoid(c)
    w = w_ref[...]
    c_hi = c.astype(BF16)
    c_lo = (c - c_hi.astype(F32)).astype(BF16)
    w_hi = w.astype(BF16)
    w_lo = (w - w_hi.astype(F32)).astype(BF16)
    acc = jnp.dot(c_hi, w_hi, preferred_element_type=F32)
    acc += jnp.dot(c_hi, w_lo, preferred_element_type=F32)
    acc += jnp.dot(c_lo, w_hi, preferred_element_type=F32)
    o_ref[...] = acc + b_ref[...]


def _adaln(c, ada_w, ada_b):
    L, D, N = ada_w.shape
    B = c.shape[0]
    tn = 1536 if N % 1536 == 0 else N
    return pl.pallas_call(
        _adaln_kernel,
        grid=(L, N // tn),
        in_specs=[
            pl.BlockSpec((B, D), lambda l, j: (0, 0)),
            pl.BlockSpec((None, D, tn), lambda l, j: (l, 0, j)),
            pl.BlockSpec((None, 1, tn), lambda l, j: (l, 0, j)),
        ],
        out_specs=pl.BlockSpec((None, B, tn), lambda l, j: (l, 0, j)),
        out_shape=jax.ShapeDtypeStruct((L, B, N), F32),
        compiler_params=_params(("arbitrary", "arbitrary")),
        name="adaln",
    )(c, ada_w, ada_b.reshape(L, 1, N))


def _inproj_kernel(x_ref, mod_ref, nw_ref, w_ref, cos_ref, sin_ref,
                   q_ref, k_ref, v_ref, u_ref, ga_ref, gs_ref, *, tm, d_model, ssm_width):
    qk_width = N_HEADS * HEAD_WIDTH
    first_half = (lax.broadcasted_iota(jnp.int32, (tm, LANES), 1) % HEAD_DIM) < (HEAD_DIM // 2)

    for st in range(x_ref.shape[0] // tm):
        rows = slice(st * tm, (st + 1) * tm)
        h = _modulated_rmsnorm(x_ref[rows, :], nw_ref[...], mod_ref[1:2, :], mod_ref[0:1, :]).astype(BF16)
        cos = cos_ref[rows, :]
        sin = sin_ref[rows, :]

        def rope_heads(col0, out_ref, scale):
            for pair in range(N_HEADS // 2):
                c0 = col0 + pair * 2 * HEAD_WIDTH
                t2 = jnp.dot(h, w_ref[:, c0:c0 + 2 * HEAD_WIDTH], preferred_element_type=F32)
                for i in range(2):
                    t = t2[:, i * HEAD_WIDTH:(i + 1) * HEAD_WIDTH]
                    partner = jnp.where(first_half,
                                        pltpu.roll(t, HEAD_WIDTH - HEAD_DIM // 2, axis=1),
                                        pltpu.roll(t, HEAD_DIM // 2, axis=1))
                    r = t * cos + partner * sin
                    if scale != 1.0:
                        r = r * scale
                    out_ref[2 * pair + i, rows, :] = r.astype(BF16)

        rope_heads(0, q_ref, LOG2_E / math.sqrt(HEAD_DIM))
        rope_heads(qk_width, k_ref, 1.0)
        for pair in range(N_HEADS // 2):
            c0 = 2 * qk_width + pair * 2 * HEAD_WIDTH
            t2 = jnp.dot(h, w_ref[:, c0:c0 + 2 * HEAD_WIDTH], preferred_element_type=F32).astype(BF16)
            v_ref[2 * pair, rows, :] = t2[:, :HEAD_WIDTH]
            v_ref[2 * pair + 1, rows, :] = t2[:, HEAD_WIDTH:]
        c0 = 3 * qk_width
        u_ref[rows, :] = jnp.dot(h, w_ref[:, c0:c0 + ssm_width], preferred_element_type=F32).astype(BF16)
        c0 += ssm_width
        ga_ref[rows, :] = jax.nn.sigmoid(
            jnp.dot(h, w_ref[:, c0:c0 + d_model], preferred_element_type=F32)).astype(BF16)
        c0 += d_model
        gs_ref[rows, :] = jax.nn.sigmoid(
            jnp.dot(h, w_ref[:, c0:c0 + d_model], preferred_element_type=F32)).astype(BF16)


def _resident(arr, *lead):
    shape = (None,) * len(lead) + arr.shape[len(lead):]
    index = tuple(lead) + (0,) * (arr.ndim - len(lead))
    return pl.BlockSpec(shape, lambda *_: index, pipeline_mode=pl.Buffered(1))


def _inproj(x, mod, norm_w, w_in, layer, cos, sin, ssm_width, tm, tiles_per_step):
    B, S, D = x.shape
    rows = tm * tiles_per_step
    head_shape = jax.ShapeDtypeStruct((B, N_HEADS, S, HEAD_WIDTH), BF16)
    head_spec = pl.BlockSpec((None, N_HEADS, rows, HEAD_WIDTH), lambda b, s: (b, 0, s, 0))
    row_spec = pl.BlockSpec((None, rows, D), lambda b, s: (b, s, 0))
    return pl.pallas_call(
        functools.partial(_inproj_kernel, tm=tm, d_model=D, ssm_width=ssm_width),
        grid=(B, S // rows),
        in_specs=[
            row_spec,
            pl.BlockSpec((None, 6, D), lambda b, s: (b, 0, 0)),
            _resident(norm_w, layer),
            _resident(w_in, layer),
            pl.BlockSpec((rows, LANES), lambda b, s: (s, 0)),
            pl.BlockSpec((rows, LANES), lambda b, s: (s, 0)),
        ],
        out_specs=[
            head_spec, head_spec, head_spec,
            pl.BlockSpec((None, rows, ssm_width), lambda b, s: (b, s, 0)),
            row_spec, row_spec,
        ],
        out_shape=[
            head_shape, head_shape, head_shape,
            jax.ShapeDtypeStruct((B, S, ssm_width), BF16),
            jax.ShapeDtypeStruct((B, S, D), BF16),
            jax.ShapeDtypeStruct((B, S, D), BF16),
        ],
        compiler_params=_params(("arbitrary", "arbitrary")),
        name="inproj",
    )(x, mod, norm_w, w_in, cos, sin)


def _attn_kernel(lam_ref, q_ref, k_ref, v_ref, sw_ref, o_ref, vaug_ref, *, tq, tk, out_scale):
    S = k_ref.shape[0]

    @pl.when(pl.program_id(2) == 0)
    def _():
        vaug_ref[:, :HEAD_WIDTH] = v_ref[...]
        vaug_ref[:, HEAD_WIDTH:] = jnp.ones((S, MXU_DIM - HEAD_WIDTH), BF16)

    for qt in range(q_ref.shape[0] // tq):
        rows = slice(qt * tq, (qt + 1) * tq)
        q = q_ref[rows, :]
        lane = lax.broadcasted_iota(jnp.int32, q.shape, 1)
        zero = jnp.zeros_like(q)
        subs = []
        for q_sub in (jnp.where(lane < HEAD_DIM, q, zero), jnp.where(lane >= HEAD_DIM, q, zero)):
            m = jnp.full((tq, 1), -jnp.inf, F32)
            acc = jnp.zeros((tq, MXU_DIM), F32)
            for c in range(S // tk):
                kc = k_ref[c * tk:(c + 1) * tk, :]
                s = lax.dot_general(q_sub, kc, (((1,), (1,)), ((), ())), preferred_element_type=F32)
                m_new = jnp.maximum(m, jnp.max(s, axis=1, keepdims=True))
                alpha = jnp.exp2(m - m_new)
                p = jnp.exp2(s - m_new).astype(BF16)
                acc = acc * alpha + jnp.dot(p, vaug_ref[c * tk:(c + 1) * tk, :], preferred_element_type=F32)
                m = m_new
            subs.append(acc[:, :HEAD_WIDTH] / acc[:, HEAD_WIDTH:])
        o = subs[0] - lam_ref[0] * subs[1]
        o = o * lax.rsqrt(jnp.mean(o * o, axis=-1, keepdims=True) + SUBLN_EPS)
        o_ref[rows, :] = ((o * sw_ref[...]) * out_scale).astype(BF16)


def _attention(q, k, v, lam, sub_w, layer, out_scale, tq, tk, tiles_per_step):
    B, H, S, W = q.shape
    rows = tq * tiles_per_step
    kv_spec = pl.BlockSpec((None, None, S, W), lambda b, h, i: (b, h, 0, 0))
    return pl.pallas_call(
        functools.partial(_attn_kernel, tq=tq, tk=tk, out_scale=out_scale),
        grid=(B, H, S // rows),
        in_specs=[
            pl.BlockSpec(memory_space=pltpu.SMEM),
            pl.BlockSpec((None, None, rows, W), lambda b, h, i: (b, h, i, 0)),
            kv_spec, kv_spec,
            _resident(sub_w, layer),
        ],
        out_specs=pl.BlockSpec((None, rows, W), lambda b, h, i: (b, i, h)),
        out_shape=jax.ShapeDtypeStruct((B, S, H * W), BF16),
        scratch_shapes=[pltpu.VMEM((S, MXU_DIM), BF16)],
        compiler_params=_params(("arbitrary", "arbitrary", "arbitrary")),
        name="diff_attn",
    )(lam.reshape(1), q, k, v, sub_w)


def _s5_discretize(lam_re, lam_im, log_dt, b_re, b_im, c_re, c_im, batch):
    lead = lam_re.shape[:-2]
    G, P = lam_re.shape[-2:]
    N = b_re.shape[-1]
    dt = jnp.exp(log_dt)[..., None]
    mag = jnp.exp(lam_re * dt)
    ang = lam_im * dt
    abar_re = mag * jnp.cos(ang)
    abar_im = mag * jnp.sin(ang)
    den = lam_re * lam_re + lam_im * lam_im
    nr = abar_re - 1.0
    ni = abar_im
    f_re = (nr * lam_re + ni * lam_im) / den
    f_im = (ni * lam_re - nr * lam_im) / den
    bb_re = f_re[..., None] * b_re - f_im[..., None] * b_im
    bb_im = f_re[..., None] * b_im + f_im[..., None] * b_re
    gh = G // 2
    nl = len(lead)

    def block_diag(m, rows, cols):
        m = m.reshape(lead + (2, gh, cols, rows))
        wide = jnp.moveaxis(m, -1, nl + 1).reshape(lead + (2, 1, rows, gh * cols))
        on_diag = (jnp.arange(gh)[:, None, None] == (jnp.arange(gh * cols) // cols)[None, None, :])
        return jnp.where(on_diag, wide, 0.0).reshape(lead + (2, gh * rows, gh * cols))

    def in_block(bb):
        return block_diag(bb, N, P)

    def out_block(cc):
        return block_diag(cc, P, N)

    b_mat = jnp.concatenate([in_block(bb_re), in_block(bb_im)], axis=-1)
    c_mat = jnp.concatenate([out_block(c_re), out_block(-c_im)], axis=-2)
    a = jnp.stack([abar_re.reshape(lead + (G * P,)), abar_im.reshape(lead + (G * P,))], axis=-2)
    a = jnp.broadcast_to(a[..., None, :], lead + (2, batch, G * P))
    return b_mat.astype(BF16), c_mat.astype(BF16), a


def _s5_kernel(uf_ref, ub_ref, bf_ref, bb_ref, cf_ref, cb_ref, af_ref, ab_ref,
               yf_ref, yb_ref, tm_ref, xf_ref, xb_ref, carf_ref, carb_ref, *, steps, batch, lane_chunk):
    half_in = uf_ref.shape[2] // 2
    half_state = af_ref.shape[2] // 2

    @pl.when(pl.program_id(0) == 0)
    def _():
        carf_ref[...] = jnp.zeros_like(carf_ref)
        carb_ref[...] = jnp.zeros_like(carb_ref)

    slabs = half_in // LANES

    for u_ref, b_ref, x_ref in ((uf_ref, bf_ref, xf_ref), (ub_ref, bb_ref, xb_ref)):
        for b in range(batch):
            ub = u_ref[b].astype(F32)
            for j in range(2 * slabs):
                tm_ref[j, pl.ds(b, steps, stride=batch), :] = ub[:, j * LANES:(j + 1) * LANES]
        for h in range(2):
            cols = slice(2 * half_state * h, 2 * half_state * (h + 1))
            lhs = jnp.concatenate([tm_ref[h * slabs + j] for j in range(slabs)], axis=1).astype(BF16)
            x_ref[:, cols] = jnp.dot(lhs, b_ref[h], preferred_element_type=F32)

    for h in range(2):
        for j in range(half_state // lane_chunk):
            re = slice(2 * half_state * h + lane_chunk * j, 2 * half_state * h + lane_chunk * (j + 1))
            im = slice(re.start + half_state, re.stop + half_state)
            ac = slice(half_state * h + lane_chunk * j, half_state * h + lane_chunk * (j + 1))
            arf, aif = af_ref[0, :, ac], af_ref[1, :, ac]
            arb, aib = ab_ref[0, :, ac], ab_ref[1, :, ac]

            fr, fi, br, bi = carf_ref[:, re], carf_ref[:, im], carb_ref[:, re], carb_ref[:, im]
            for t in range(steps):
                rf = slice(t * batch, (t + 1) * batch)
                rb = slice((steps - 1 - t) * batch, (steps - t) * batch)
                fr, fi = arf * fr - aif * fi + xf_ref[rf, re], arf * fi + aif * fr + xf_ref[rf, im]
                br, bi = arb * br - aib * bi + xb_ref[rb, re], arb * bi + aib * br + xb_ref[rb, im]
                xf_ref[rf, re] = fr
                xf_ref[rf, im] = fi
                xb_ref[rb, re] = br
                xb_ref[rb, im] = bi
            carf_ref[:, re] = fr
            carf_ref[:, im] = fi
            carb_ref[:, re] = br
            carb_ref[:, im] = bi

    for x_ref, c_ref, y_ref in ((xf_ref, cf_ref, yf_ref), (xb_ref, cb_ref, yb_ref)):
        for h in range(2):
            cols = slice(2 * half_state * h, 2 * half_state * (h + 1))
            y = jnp.dot(x_ref[:, cols].astype(BF16), c_ref[h], preferred_element_type=F32)
            for j in range(slabs):
                tm_ref[h * slabs + j] = y[:, j * LANES:(j + 1) * LANES]
        for b in range(batch):
            for j in range(2 * slabs):
                y_ref[b, :, j * LANES:(j + 1) * LANES] = tm_ref[j, pl.ds(b, steps, stride=batch), :]


def _s5(u, b_mat, c_mat, a, layer, steps):
    batch, S, W = u.shape
    rows = steps * batch
    n = S // steps
    n_state = a.shape[-1]
    return pl.pallas_call(
        functools.partial(_s5_kernel, steps=steps, batch=batch, lane_chunk=512),
        grid=(n,),
        in_specs=[
            pl.BlockSpec((batch, steps, W), lambda i: (0, i, 0)),
            pl.BlockSpec((batch, steps, W), lambda i: (0, n - 1 - i, 0)),
            _resident(b_mat, layer, 0), _resident(b_mat, layer, 1),
            _resident(c_mat, layer, 0), _resident(c_mat, layer, 1),
            _resident(a, layer, 0), _resident(a, layer, 1),
        ],
        out_specs=[
            pl.BlockSpec((batch, steps, W), lambda i: (0, i, 0)),
            pl.BlockSpec((batch, steps, W), lambda i: (0, n - 1 - i, 0)),
        ],
        out_shape=[jax.ShapeDtypeStruct((batch, S, W), F32)] * 2,
        scratch_shapes=[
            pltpu.VMEM((W // LANES, rows, LANES), F32),
            pltpu.VMEM((rows, 2 * n_state), F32), pltpu.VMEM((rows, 2 * n_state), F32),
            pltpu.VMEM((batch, 2 * n_state), F32), pltpu.VMEM((batch, 2 * n_state), F32),
        ],
        compiler_params=_params(("arbitrary",)),
        name="s5_scan",
    )(u, u, b_mat, b_mat, c_mat, c_mat, a, a)


def _gelu_tanh(y):
    return 0.5 * y * (1.0 + jnp.tanh(math.sqrt(2.0 / math.pi) * (y + 0.044715 * (y * y * y))))


def _merge_kernel(x_ref, mod_ref, oa_ref, yf_ref, yb_ref, u_ref, d_ref, ga_ref, gs_ref,
                  wao_ref, wglu_ref, wout_ref, o_ref, *, tm, d_model):
    for st in range(x_ref.shape[0] // tm):
        rows = slice(st * tm, (st + 1) * tm)
        y = yf_ref[rows, :] + yb_ref[rows, :] + d_ref[...] * u_ref[rows, :].astype(F32)
        y = _gelu_tanh(y).astype(BF16)
        ab = jnp.dot(y, wglu_ref[...], preferred_element_type=F32)
        o_ssm = ab[:, :d_model] * jax.nn.sigmoid(ab[:, d_model:])
        o_attn = jnp.dot(oa_ref[rows, :], wao_ref[...], preferred_element_type=F32)
        merged = ga_ref[rows, :].astype(F32) * o_attn + gs_ref[rows, :].astype(F32) * o_ssm
        out = jnp.dot(merged.astype(BF16), wout_ref[...], preferred_element_type=F32)
        o_ref[rows, :] = x_ref[rows, :] + mod_ref[2:3, :] * out


def _merge(x, mod, o_attn, yf, yb, u, ssm_d, ga, gs, w_attn_out, w_glu, w_out, layer, tm, tiles_per_step):
    B, S, D = x.shape
    W = ssm_d.shape[-1]
    rows = tm * tiles_per_step
    row_spec = pl.BlockSpec((None, rows, D), lambda b, s: (b, s, 0))
    ssm_spec = pl.BlockSpec((None, rows, W), lambda b, s: (b, s, 0))
    return pl.pallas_call(
        functools.partial(_merge_kernel, tm=tm, d_model=D),
        grid=(B, S // rows),
        in_specs=[
            row_spec,
            pl.BlockSpec((None, 6, D), lambda b, s: (b, 0, 0)),
            row_spec, ssm_spec, ssm_spec, ssm_spec,
            _resident(ssm_d, layer),
            row_spec, row_spec,
            _resident(w_attn_out, layer), _resident(w_glu, layer), _resident(w_out, layer),
        ],
        out_specs=row_spec,
        out_shape=jax.ShapeDtypeStruct((B, S, D), F32),
        compiler_params=_params(("arbitrary", "arbitrary")),
        name="merge",
    )(x, mod, o_attn, yf, yb, u, ssm_d, ga, gs, w_attn_out, w_glu, w_out)


def _ffn_kernel(x_ref, xp_ref, xn_ref, mod_ref, nw_ref, wup_ref, cw_ref, cb_ref, wdn_ref, fw_ref,
                o_ref, hp_ref, up_ref, act_ref, *, tm, d_ff, ff_chunk, final_norm):
    n_sub = x_ref.shape[0] // tm
    seg = tm // SUBLANES
    body = slice(SUBLANES, SUBLANES + tm)
    si = pl.program_id(1)
    keep_prev = (si > 0).astype(F32)
    keep_next = (si < pl.num_programs(1) - 1).astype(F32)
    nw, scale, shift = nw_ref[...], mod_ref[4:5, :], mod_ref[3:4, :]
    slabs = hp_ref.shape[1]
    sub = lax.broadcasted_iota(jnp.int32, (SUBLANES, ff_chunk), 0)

    for st in range(n_sub):
        t0 = st * tm

        def put(rows, val):
            for j in range(slabs):
                hp_ref[st, j, rows, :] = val[:, j * LANES:(j + 1) * LANES]

        if st == 0:
            h_prev = _modulated_rmsnorm(xp_ref[...], nw, scale, shift) * keep_prev
        else:
            h_prev = _modulated_rmsnorm(x_ref[t0 - SUBLANES:t0, :], nw, scale, shift)
        if st == n_sub - 1:
            h_next = _modulated_rmsnorm(xn_ref[...], nw, scale, shift) * keep_next
        else:
            h_next = _modulated_rmsnorm(x_ref[t0 + tm:t0 + tm + SUBLANES, :], nw, scale, shift)
        put(slice(0, SUBLANES), h_prev)
        put(slice(SUBLANES + tm, 2 * SUBLANES + tm), h_next)
        for r in range(SUBLANES):
            put(pl.ds(SUBLANES + r, seg, stride=SUBLANES),
                _modulated_rmsnorm(x_ref[t0 + r * seg:t0 + (r + 1) * seg, :], nw, scale, shift))
        h = jnp.concatenate([hp_ref[st, j] for j in range(slabs)], axis=1).astype(BF16)

        def conv(col0):
            up_ref[...] = jnp.dot(h, wup_ref[:, col0:col0 + ff_chunk], preferred_element_type=F32)
            up_ref[0:SUBLANES, :] = jnp.where(
                sub == 0, pltpu.roll(up_ref[0:SUBLANES, :], 1, axis=0),
                pltpu.roll(up_ref[tm:SUBLANES + tm, :], 1, axis=0))
            up_ref[SUBLANES + tm:, :] = jnp.where(
                sub == SUBLANES - 1, pltpu.roll(up_ref[SUBLANES + tm:, :], SUBLANES - 1, axis=0),
                pltpu.roll(up_ref[SUBLANES:2 * SUBLANES, :], SUBLANES - 1, axis=0))
            cw = cw_ref[:, col0:col0 + ff_chunk]
            return (up_ref[0:tm, :] * cw[0:1, :]
                    + up_ref[body, :] * cw[1:2, :]
                    + up_ref[2 * SUBLANES:2 * SUBLANES + tm, :] * cw[2:3, :]
                    + cb_ref[:, col0:col0 + ff_chunk])

        for c in range(d_ff // ff_chunk):
            val = conv(c * ff_chunk)
            gt = conv(d_ff + c * ff_chunk)
            act_ref[st, :, c * ff_chunk:(c + 1) * ff_chunk] = ((gt * jax.nn.sigmoid(gt)) * val).astype(BF16)

        put(body, mod_ref[5:6, :] * jnp.dot(act_ref[st], wdn_ref[...], preferred_element_type=F32))
        for r in range(SUBLANES):
            rows = pl.ds(SUBLANES + r, seg, stride=SUBLANES)
            xo = x_ref[t0 + r * seg:t0 + (r + 1) * seg, :] + jnp.concatenate(
                [hp_ref[st, j, rows, :] for j in range(slabs)], axis=1)
            if final_norm:
                xo = (xo * lax.rsqrt(jnp.mean(xo * xo, axis=-1, keepdims=True) + EPS)) * fw_ref[...]
            o_ref[t0 + r * seg:t0 + (r + 1) * seg, :] = xo


def _ffn(x, mod, norm_w, w_up, conv_w, conv_b, w_down, final_w, layer, final_norm, tm, tiles_per_step):
    B, S, D = x.shape
    d_ff = w_down.shape[-2]
    ff_chunk = _ff_chunk(d_ff)
    rows = tm * tiles_per_step
    halo_blocks = rows // SUBLANES
    n_halo = S // SUBLANES
    row_spec = pl.BlockSpec((None, rows, D), lambda b, s: (b, s, 0))
    final_w = final_w.reshape(1, D)
    return pl.pallas_call(
        functools.partial(_ffn_kernel, tm=tm, d_ff=d_ff, ff_chunk=ff_chunk, final_norm=final_norm),
        grid=(B, S // rows),
        in_specs=[
            row_spec,
            pl.BlockSpec((None, SUBLANES, D), lambda b, s: (b, jnp.maximum(s * halo_blocks - 1, 0), 0)),
            pl.BlockSpec((None, SUBLANES, D),
                         lambda b, s: (b, jnp.minimum((s + 1) * halo_blocks, n_halo - 1), 0)),
            pl.BlockSpec((None, 6, D), lambda b, s: (b, 0, 0)),
            _resident(norm_w, layer),
            _resident(w_up, layer), _resident(conv_w, layer), _resident(conv_b, layer), _resident(w_down, layer),
            _resident(final_w),
        ],
        out_specs=row_spec,
        out_shape=jax.ShapeDtypeStruct((B, S, D), F32),
        scratch_shapes=[
            pltpu.VMEM((tiles_per_step, D // LANES, tm + 2 * SUBLANES, LANES), F32),
            pltpu.VMEM((tm + 2 * SUBLANES, ff_chunk), F32),
            pltpu.VMEM((tiles_per_step, tm, d_ff), BF16),
        ],
        compiler_params=_params(("arbitrary", "arbitrary")),
        name="convffn",
    )(x, x, x, mod, norm_w, w_up, conv_w, conv_b, w_down, final_w)


def _rope_tables(seq):
    half = HEAD_DIM // 2
    inv_freq = ROPE_THETA ** (-jnp.arange(0, half, dtype=F32) / half)
    ang = jnp.arange(seq, dtype=F32)[:, None] * inv_freq[None, :]
    cos, sin = jnp.cos(ang), jnp.sin(ang)
    reps = LANES // HEAD_DIM
    cos_t = jnp.tile(jnp.concatenate([cos, cos], axis=1), (1, reps))
    sin_t = jnp.tile(jnp.concatenate([-sin, sin], axis=1), (1, reps))
    return cos_t, sin_t


def _ff_chunk(d_ff):
    return MXU_DIM if d_ff % MXU_DIM == 0 else d_ff


def _pick(size, pref):
    return pref if size % pref == 0 else size


def kernel(x, c, norm1_w, norm2_w, ada_w, ada_b, w_in, lam_q1, lam_k1, lam_q2, lam_k2, subln_w, w_attn_out, ssm_lam_re, ssm_lam_im, ssm_log_dt, ssm_b_re, ssm_b_im, ssm_c_re, ssm_c_im, ssm_d, w_glu_a, w_glu_b, w_out, w_up, conv_w, conv_b, w_down, final_w):
    B, S, D = x.shape
    depth = ada_w.shape[0]
    ssm_width = ssm_d.shape[1]
    assert B == SUBLANES, "the S5 scan maps the batch onto the 8 sublanes of a vreg"
    assert w_in.shape[2] == 3 * N_HEADS * HEAD_WIDTH + ssm_width + 2 * D
    tm = _pick(S, 512)
    tq = _pick(S, 1024)
    tk = _pick(S, 1024)
    steps = _pick(S, 64)
    row_tiles = 2 if S % (2 * tm) == 0 else 1
    q_tiles = 2 if S % (2 * tq) == 0 else 1

    cos_t, sin_t = _rope_tables(S)
    mod_all = _adaln(c, ada_w, ada_b).reshape(depth, B, 6, D)
    w_in_b, w_up_b, w_down_b = w_in.astype(BF16), w_up.astype(BF16), w_down.astype(BF16)
    w_attn_out_b, w_out_b = w_attn_out.astype(BF16), w_out.astype(BF16)
    w_glu_b16 = jnp.concatenate([w_glu_a, w_glu_b], axis=2).astype(BF16)
    norm1, norm2 = norm1_w.reshape(depth, 1, D), norm2_w.reshape(depth, 1, D)
    sub_w = subln_w.reshape(depth, 1, HEAD_WIDTH)
    d_skip = ssm_d.reshape(depth, 1, ssm_width)
    conv_b3 = conv_b.reshape(depth, 1, -1)
    b_mat, c_mat, a_bar = _s5_discretize(ssm_lam_re, ssm_lam_im, ssm_log_dt, ssm_b_re, ssm_b_im,
                                         ssm_c_re, ssm_c_im, B)
    lam_all = (jnp.exp(jnp.sum(lam_q1 * lam_k1, axis=-1)) - jnp.exp(jnp.sum(lam_q2 * lam_k2, axis=-1))).astype(F32)

    for l in range(depth):
        lambda_init = 0.8 - 0.6 * math.exp(-0.3 * l)
        mod = mod_all[l]
        q, k, v, u, ga, gs = _inproj(x, mod, norm1, w_in_b, l, cos_t, sin_t, ssm_width, tm, row_tiles)
        o_attn = _attention(q, k, v, lam_all[l] + lambda_init, sub_w, l, 1.0 - lambda_init, tq, tk, q_tiles)
        yf, yb = _s5(u, b_mat, c_mat, a_bar, l, steps)
        x = _merge(x, mod, o_attn, yf, yb, u, d_skip, ga, gs, w_attn_out_b, w_glu_b16, w_out_b, l, tm, row_tiles)
        x = _ffn(x, mod, norm2, w_up_b, conv_w, conv_b3, w_down_b, final_w, l, l == depth - 1, tm, row_tiles)
    return x
```

```python
import functools
import math

import jax
import jax.numpy as jnp
from jax import lax
from jax.experimental import pallas as pl
from jax.experimental.pallas import tpu as pltpu

N_HEADS = 8
HEAD_DIM = 64
HEAD_WIDTH = 2 * HEAD_DIM
ROPE_THETA = 10000.0
SSM_GROUP = 16
SSM_STATE = 64
CONV_WIDTH = 3
EPS = 1e-6
SUBLN_EPS = 1e-5
LOG2_E = math.log2(math.e)

LANES = 128
SUBLANES = 8
MXU_DIM = 256
VMEM_LIMIT_BYTES = 56 * 1024 * 1024

BF16 = jnp.bfloat16
F32 = jnp.float32


def _params(semantics):
    return pltpu.CompilerParams(dimension_semantics=semantics, vmem_limit_bytes=VMEM_LIMIT_BYTES)


def _modulated_rmsnorm(x, w, scale, shift):
    y = x * lax.rsqrt(jnp.mean(x * x, axis=-1, keepdims=True) + EPS)
    return (y * w) * (1.0 + scale) + shift


def _adaln_kernel(c_ref, w_ref, b_ref, o_ref):
    c = c_ref[...]
    c = c * jax.nn.sigmoid(c)
    w = w_ref[...]
    c_hi = c.astype(BF16)
    c_lo = (c - c_hi.astype(F32)).astype(BF16)
    w_hi = w.astype(BF16)
    w_lo = (w - w_hi.astype(F32)).astype(BF16)
    acc = jnp.dot(c_hi, w_hi, preferred_element_type=F32)
    acc += jnp.dot(c_hi, w_lo, preferred_element_type=F32)
    acc += jnp.dot(c_lo, w_hi, preferred_element_type=F32)
    o_ref[...] = acc + b_ref[...]


def _adaln(c, ada_w, ada_b):
    L, D, N = ada_w.shape
    B = c.shape[0]
    tn = 1536 if N % 1536 == 0 else N
    return pl.pallas_call(
        _adaln_kernel,
        grid=(L, N // tn),
        in_specs=[
            pl.BlockSpec((B, D), lambda l, j: (0, 0)),
            pl.BlockSpec((None, D, tn), lambda l, j: (l, 0, j)),
            pl.BlockSpec((None, 1, tn), lambda l, j: (l, 0, j)),
        ],
        out_specs=pl.BlockSpec((None, B, tn), lambda l, j: (l, 0, j)),
        out_shape=jax.ShapeDtypeStruct((L, B, N), F32),
        compiler_params=_params(("arbitrary", "arbitrary")),
        name="adaln",
    )(c, ada_w, ada_b.reshape(L, 1, N))


def _inproj_kernel(x_ref, mod_ref, nw_ref, w_ref, cos_ref, sin_ref,
                   q_ref, k_ref, v_ref, u_ref, ga_ref, gs_ref, *, tm, d_model, ssm_width):
    qk_width = N_HEADS * HEAD_WIDTH
    first_half = (lax.broadcasted_iota(jnp.int32, (tm, LANES), 1) % HEAD_DIM) < (HEAD_DIM // 2)

    for st in range(x_ref.shape[0] // tm):
        rows = slice(st * tm, (st + 1) * tm)
        h = _modulated_rmsnorm(x_ref[rows, :], nw_ref[...], mod_ref[1:2, :], mod_ref[0:1, :]).astype(BF16)
        cos = cos_ref[rows, :]
        sin = sin_ref[rows, :]

        def rope_heads(col0, out_ref, scale):
            for pair in range(N_HEADS // 2):
                c0 = col0 + pair * 2 * HEAD_WIDTH
                t2 = jnp.dot(h, w_ref[:, c0:c0 + 2 * HEAD_WIDTH], preferred_element_type=F32)
                for i in range(2):
                    t = t2[:, i * HEAD_WIDTH:(i + 1) * HEAD_WIDTH]
                    partner = jnp.where(first_half,
                                        pltpu.roll(t, HEAD_WIDTH - HEAD_DIM // 2, axis=1),
                                        pltpu.roll(t, HEAD_DIM // 2, axis=1))
                    r = t * cos + partner * sin
                    if scale != 1.0:
                        r = r * scale
                    out_ref[2 * pair + i, rows, :] = r.astype(BF16)

        rope_heads(0, q_ref, LOG2_E / math.sqrt(HEAD_DIM))
        rope_heads(qk_width, k_ref, 1.0)
        for pair in range(N_HEADS // 2):
            c0 = 2 * qk_width + pair * 2 * HEAD_WIDTH
            t2 = jnp.dot(h, w_ref[:, c0:c0 + 2 * HEAD_WIDTH], preferred_element_type=F32).astype(BF16)
            v_ref[2 * pair, rows, :] = t2[:, :HEAD_WIDTH]
            v_ref[2 * pair + 1, rows, :] = t2[:, HEAD_WIDTH:]
        c0 = 3 * qk_width
        u_ref[rows, :] = jnp.dot(h, w_ref[:, c0:c0 + ssm_width], preferred_element_type=F32).astype(BF16)
        c0 += ssm_width
        ga_ref[rows, :] = jax.nn.sigmoid(
            jnp.dot(h, w_ref[:, c0:c0 + d_model], preferred_element_type=F32)).astype(BF16)
        c0 += d_model
        gs_ref[rows, :] = jax.nn.sigmoid(
            jnp.dot(h, w_ref[:, c0:c0 + d_model], preferred_element_type=F32)).astype(BF16)


def _resident(arr, *lead):
    shape = (None,) * len(lead) + arr.shape[len(lead):]
    index = tuple(lead) + (0,) * (arr.ndim - len(lead))
    return pl.BlockSpec(shape, lambda *_: index, pipeline_mode=pl.Buffered(1))


def _inproj(x, mod, norm_w, w_in, layer, cos, sin, ssm_width, tm, tiles_per_step):
    B, S, D = x.shape
    rows = tm * tiles_per_step
    head_shape = jax.ShapeDtypeStruct((B, N_HEADS, S, HEAD_WIDTH), BF16)
    head_spec = pl.BlockSpec((None, N_HEADS, rows, HEAD_WIDTH), lambda b, s: (b, 0, s, 0))
    row_spec = pl.BlockSpec((None, rows, D), lambda b, s: (b, s, 0))
    return pl.pallas_call(
        functools.partial(_inproj_kernel, tm=tm, d_model=D, ssm_width=ssm_width),
        grid=(B, S // rows),
        in_specs=[
            row_spec,
            pl.BlockSpec((None, 6, D), lambda b, s: (b, 0, 0)),
            _resident(norm_w, layer),
            _resident(w_in, layer),
            pl.BlockSpec((rows, LANES), lambda b, s: (s, 0)),
            pl.BlockSpec((rows, LANES), lambda b, s: (s, 0)),
        ],
        out_specs=[
            head_spec, head_spec, head_spec,
            pl.BlockSpec((None, rows, ssm_width), lambda b, s: (b, s, 0)),
            row_spec, row_spec,
        ],
        out_shape=[
            head_shape, head_shape, head_shape,
            jax.ShapeDtypeStruct((B, S, ssm_width), BF16),
            jax.ShapeDtypeStruct((B, S, D), BF16),
            jax.ShapeDtypeStruct((B, S, D), BF16),
        ],
        compiler_params=_params(("arbitrary", "arbitrary")),
        name="inproj",
    )(x, mod, norm_w, w_in, cos, sin)


def _attn_kernel(lam_ref, q_ref, k_ref, v_ref, sw_ref, o_ref, vaug_ref, *, tq, tk, out_scale):
    S = k_ref.shape[0]

    @pl.when(pl.program_id(2) == 0)
    def _():
        vaug_ref[:, :HEAD_WIDTH] = v_ref[...]
        vaug_ref[:, HEAD_WIDTH:] = jnp.ones((S, MXU_DIM - HEAD_WIDTH), BF16)

    n_tiles = q_ref.shape[0] // tq
    for qt in range(n_tiles):
        rows = slice(qt * tq, (qt + 1) * tq)
        q = q_ref[rows, :]
        lane = lax.broadcasted_iota(jnp.int32, q.shape, 1)
        zero = jnp.zeros_like(q)
        subs = []
        for i, q_sub in enumerate((jnp.where(lane < HEAD_DIM, q, zero), jnp.where(lane >= HEAD_DIM, q, zero))):
            m = jnp.full((tq, 1), -jnp.inf, F32)
            acc = jnp.zeros((tq, MXU_DIM), F32)
            edges = list(range(0, S + 1, tk))
            if qt == 0 and i == 0 and tk >= 4 * LANES:
                edges.insert(1, tk // 4)
            if qt == n_tiles - 1 and i == 1 and tk >= 4 * LANES:
                edges.insert(-1, S - tk // 4)
            for k0, k1 in zip(edges[:-1], edges[1:]):
                kc = k_ref[k0:k1, :]
                s = lax.dot_general(q_sub, kc, (((1,), (1,)), ((), ())), preferred_element_type=F32)
                m_new = jnp.maximum(m, jnp.max(s, axis=1, keepdims=True))
                alpha = jnp.exp2(m - m_new)
                p = jnp.exp2(s - m_new).astype(BF16)
                acc = acc * alpha + jnp.dot(p, vaug_ref[k0:k1, :], preferred_element_type=F32)
                m = m_new
            subs.append(acc[:, :HEAD_WIDTH] / acc[:, HEAD_WIDTH:])
        o = subs[0] - lam_ref[0] * subs[1]
        o = o * lax.rsqrt(jnp.mean(o * o, axis=-1, keepdims=True) + SUBLN_EPS)
        o_ref[rows, :] = ((o * sw_ref[...]) * out_scale).astype(BF16)


def _attention(q, k, v, lam, sub_w, layer, out_scale, tq, tk, tiles_per_step):
    B, H, S, W = q.shape
    rows = tq * tiles_per_step
    kv_spec = pl.BlockSpec((None, None, S, W), lambda b, h, i: (b, h, 0, 0))
    return pl.pallas_call(
        functools.partial(_attn_kernel, tq=tq, tk=tk, out_scale=out_scale),
        grid=(B, H, S // rows),
        in_specs=[
            pl.BlockSpec(memory_space=pltpu.SMEM),
            pl.BlockSpec((None, None, rows, W), lambda b, h, i: (b, h, i, 0)),
            kv_spec, kv_spec,
            _resident(sub_w, layer),
        ],
        out_specs=pl.BlockSpec((None, rows, W), lambda b, h, i: (b, i, h)),
        out_shape=jax.ShapeDtypeStruct((B, S, H * W), BF16),
        scratch_shapes=[pltpu.VMEM((S, MXU_DIM), BF16)],
        compiler_params=_params(("arbitrary", "arbitrary", "arbitrary")),
        name="diff_attn",
    )(lam.reshape(1), q, k, v, sub_w)


def _s5_discretize(lam_re, lam_im, log_dt, b_re, b_im, c_re, c_im, batch):
    lead = lam_re.shape[:-2]
    G, P = lam_re.shape[-2:]
    N = b_re.shape[-1]
    dt = jnp.exp(log_dt)[..., None]
    mag = jnp.exp(lam_re * dt)
    ang = lam_im * dt
    abar_re = mag * jnp.cos(ang)
    abar_im = mag * jnp.sin(ang)
    den = lam_re * lam_re + lam_im * lam_im
    nr = abar_re - 1.0
    ni = abar_im
    f_re = (nr * lam_re + ni * lam_im) / den
    f_im = (ni * lam_re - nr * lam_im) / den
    bb_re = f_re[..., None] * b_re - f_im[..., None] * b_im
    bb_im = f_re[..., None] * b_im + f_im[..., None] * b_re
    gh = G // 2
    nl = len(lead)

    def block_diag(m, rows, cols):
        m = m.reshape(lead + (2, gh, cols, rows))
        wide = jnp.moveaxis(m, -1, nl + 1).reshape(lead + (2, 1, rows, gh * cols))
        on_diag = (jnp.arange(gh)[:, None, None] == (jnp.arange(gh * cols) // cols)[None, None, :])
        return jnp.where(on_diag, wide, 0.0).reshape(lead + (2, gh * rows, gh * cols))

    def in_block(bb):
        return block_diag(bb, N, P)

    def out_block(cc):
        return block_diag(cc, P, N)

    b_mat = jnp.concatenate([in_block(bb_re), in_block(bb_im)], axis=-1)
    c_mat = jnp.concatenate([out_block(c_re), out_block(-c_im)], axis=-2)
    a = jnp.stack([abar_re.reshape(lead + (G * P,)), abar_im.reshape(lead + (G * P,))], axis=-2)
    a = jnp.broadcast_to(a[..., None, :], lead + (2, batch, G * P))
    return b_mat.astype(BF16), c_mat.astype(BF16), a


def _s5_kernel(uf_ref, ub_ref, bf_ref, bb_ref, cf_ref, cb_ref, af_ref, ab_ref,
               yf_ref, yb_ref, tm_ref, xf_ref, xb_ref, carf_ref, carb_ref, *, steps, batch, lane_chunk):
    half_in = uf_ref.shape[2] // 2
    half_state = af_ref.shape[2] // 2

    @pl.when(pl.program_id(0) == 0)
    def _():
        carf_ref[...] = jnp.zeros_like(carf_ref)
        carb_ref[...] = jnp.zeros_like(carb_ref)

    slabs = half_in // LANES

    for u_ref, b_ref, x_ref in ((uf_ref, bf_ref, xf_ref), (ub_ref, bb_ref, xb_ref)):
        for b in range(batch):
            ub = u_ref[b].astype(F32)
            for j in range(2 * slabs):
                tm_ref[j, pl.ds(b, steps, stride=batch), :] = ub[:, j * LANES:(j + 1) * LANES]
        for h in range(2):
            cols = slice(2 * half_state * h, 2 * half_state * (h + 1))
            lhs = jnp.concatenate([tm_ref[h * slabs + j] for j in range(slabs)], axis=1).astype(BF16)
            x_ref[:, cols] = jnp.dot(lhs, b_ref[h], preferred_element_type=F32)

    for h in range(2):
        for j in range(half_state // lane_chunk):
            re = slice(2 * half_state * h + lane_chunk * j, 2 * half_state * h + lane_chunk * (j + 1))
            im = slice(re.start + half_state, re.stop + half_state)
            ac = slice(half_state * h + lane_chunk * j, half_state * h + lane_chunk * (j + 1))
            arf, aif = af_ref[0, :, ac], af_ref[1, :, ac]
            arb, aib = ab_ref[0, :, ac], ab_ref[1, :, ac]

            fr, fi, br, bi = carf_ref[:, re], carf_ref[:, im], carb_ref[:, re], carb_ref[:, im]
            for t in range(steps):
                rf = slice(t * batch, (t + 1) * batch)
                rb = slice((steps - 1 - t) * batch, (steps - t) * batch)
                fr, fi = arf * fr - aif * fi + xf_ref[rf, re], arf * fi + aif * fr + xf_ref[rf, im]
                br, bi = arb * br - aib * bi + xb_ref[rb, re], arb * bi + aib * br + xb_ref[rb, im]
                xf_ref[rf, re] = fr
                xf_ref[rf, im] = fi
                xb_ref[rb, re] = br
                xb_ref[rb, im] = bi
            carf_ref[:, re] = fr
            carf_ref[:, im] = fi
            carb_ref[:, re] = br
            carb_ref[:, im] = bi

    for x_ref, c_ref, y_ref in ((xf_ref, cf_ref, yf_ref), (xb_ref, cb_ref, yb_ref)):
        for h in range(2):
            cols = slice(2 * half_state * h, 2 * half_state * (h + 1))
            y = jnp.dot(x_ref[:, cols].astype(BF16), c_ref[h], preferred_element_type=F32)
            for j in range(slabs):
                tm_ref[h * slabs + j] = y[:, j * LANES:(j + 1) * LANES]
        for b in range(batch):
            for j in range(2 * slabs):
                y_ref[b, :, j * LANES:(j + 1) * LANES] = tm_ref[j, pl.ds(b, steps, stride=batch), :]


def _s5(u, b_mat, c_mat, a, layer, steps):
    batch, S, W = u.shape
    rows = steps * batch
    n = S // steps
    n_state = a.shape[-1]
    return pl.pallas_call(
        functools.partial(_s5_kernel, steps=steps, batch=batch, lane_chunk=512),
        grid=(n,),
        in_specs=[
            pl.BlockSpec((batch, steps, W), lambda i: (0, i, 0)),
            pl.BlockSpec((batch, steps, W), lambda i: (0, n - 1 - i, 0)),
            _resident(b_mat, layer, 0), _resident(b_mat, layer, 1),
            _resident(c_mat, layer, 0), _resident(c_mat, layer, 1),
            _resident(a, layer, 0), _resident(a, layer, 1),
        ],
        out_specs=[
            pl.BlockSpec((batch, steps, W), lambda i: (0, i, 0)),
            pl.BlockSpec((batch, steps, W), lambda i: (0, n - 1 - i, 0)),
        ],
        out_shape=[jax.ShapeDtypeStruct((batch, S, W), F32)] * 2,
        scratch_shapes=[
            pltpu.VMEM((W // LANES, rows, LANES), F32),
            pltpu.VMEM((rows, 2 * n_state), F32), pltpu.VMEM((rows, 2 * n_state), F32),
            pltpu.VMEM((batch, 2 * n_state), F32), pltpu.VMEM((batch, 2 * n_state), F32),
        ],
        compiler_params=_params(("arbitrary",)),
        name="s5_scan",
    )(u, u, b_mat, b_mat, c_mat, c_mat, a, a)


def _gelu_tanh(y):
    return 0.5 * y * (1.0 + jnp.tanh(math.sqrt(2.0 / math.pi) * (y + 0.044715 * (y * y * y))))


def _merge_kernel(x_ref, mod_ref, oa_ref, yf_ref, yb_ref, u_ref, d_ref, ga_ref, gs_ref,
                  wao_ref, wglu_ref, wout_ref, o_ref, *, tm, d_model):
    for st in range(x_ref.shape[0] // tm):
        rows = slice(st * tm, (st + 1) * tm)
        y = yf_ref[rows, :] + yb_ref[rows, :] + d_ref[...] * u_ref[rows, :].astype(F32)
        y = _gelu_tanh(y).astype(BF16)
        ab = jnp.dot(y, wglu_ref[...], preferred_element_type=F32)
        o_ssm = ab[:, :d_model] * jax.nn.sigmoid(ab[:, d_model:])
        o_attn = jnp.dot(oa_ref[rows, :], wao_ref[...], preferred_element_type=F32)
        merged = ga_ref[rows, :].astype(F32) * o_attn + gs_ref[rows, :].astype(F32) * o_ssm
        out = jnp.dot(merged.astype(BF16), wout_ref[...], preferred_element_type=F32)
        o_ref[rows, :] = x_ref[rows, :] + mod_ref[2:3, :] * out


def _merge(x, mod, o_attn, yf, yb, u, ssm_d, ga, gs, w_attn_out, w_glu, w_out, layer, tm, tiles_per_step):
    B, S, D = x.shape
    W = ssm_d.shape[-1]
    rows = tm * tiles_per_step
    row_spec = pl.BlockSpec((None, rows, D), lambda b, s: (b, s, 0))
    ssm_spec = pl.BlockSpec((None, rows, W), lambda b, s: (b, s, 0))
    return pl.pallas_call(
        functools.partial(_merge_kernel, tm=tm, d_model=D),
        grid=(B, S // rows),
        in_specs=[
            row_spec,
            pl.BlockSpec((None, 6, D), lambda b, s: (b, 0, 0)),
            row_spec, ssm_spec, ssm_spec, ssm_spec,
            _resident(ssm_d, layer),
            row_spec, row_spec,
            _resident(w_attn_out, layer), _resident(w_glu, layer), _resident(w_out, layer),
        ],
        out_specs=row_spec,
        out_shape=jax.ShapeDtypeStruct((B, S, D), F32),
        compiler_params=_params(("arbitrary", "arbitrary")),
        name="merge",
    )(x, mod, o_attn, yf, yb, u, ssm_d, ga, gs, w_attn_out, w_glu, w_out)


def _ffn_kernel(x_ref, xp_ref, xn_ref, mod_ref, nw_ref, wup_ref, cw_ref, cb_ref, wdn_ref, fw_ref,
                o_ref, hp_ref, up_ref, act_ref, *, tm, d_ff, ff_chunk, final_norm):
    n_sub = x_ref.shape[0] // tm
    seg = tm // SUBLANES
    body = slice(SUBLANES, SUBLANES + tm)
    si = pl.program_id(1)
    keep_prev = (si > 0).astype(F32)
    keep_next = (si < pl.num_programs(1) - 1).astype(F32)
    nw, scale, shift = nw_ref[...], mod_ref[4:5, :], mod_ref[3:4, :]
    slabs = hp_ref.shape[1]
    sub = lax.broadcasted_iota(jnp.int32, (SUBLANES, ff_chunk), 0)

    for st in range(n_sub):
        t0 = st * tm

        def put(rows, val):
            for j in range(slabs):
                hp_ref[st, j, rows, :] = val[:, j * LANES:(j + 1) * LANES]

        if st == 0:
            h_prev = _modulated_rmsnorm(xp_ref[...], nw, scale, shift) * keep_prev
        else:
            h_prev = _modulated_rmsnorm(x_ref[t0 - SUBLANES:t0, :], nw, scale, shift)
        if st == n_sub - 1:
            h_next = _modulated_rmsnorm(xn_ref[...], nw, scale, shift) * keep_next
        else:
            h_next = _modulated_rmsnorm(x_ref[t0 + tm:t0 + tm + SUBLANES, :], nw, scale, shift)
        put(slice(0, SUBLANES), h_prev)
        put(slice(SUBLANES + tm, 2 * SUBLANES + tm), h_next)
        for r in range(SUBLANES):
            put(pl.ds(SUBLANES + r, seg, stride=SUBLANES),
                _modulated_rmsnorm(x_ref[t0 + r * seg:t0 + (r + 1) * seg, :], nw, scale, shift))
        h = jnp.concatenate([hp_ref[st, j] for j in range(slabs)], axis=1).astype(BF16)

        def conv(col0):
            up_ref[...] = jnp.dot(h, wup_ref[:, col0:col0 + ff_chunk], preferred_element_type=F32)
            up_ref[0:SUBLANES, :] = jnp.where(
                sub == 0, pltpu.roll(up_ref[0:SUBLANES, :], 1, axis=0),
                pltpu.roll(up_ref[tm:SUBLANES + tm, :], 1, axis=0))
            up_ref[SUBLANES + tm:, :] = jnp.where(
                sub == SUBLANES - 1, pltpu.roll(up_ref[SUBLANES + tm:, :], SUBLANES - 1, axis=0),
                pltpu.roll(up_ref[SUBLANES:2 * SUBLANES, :], SUBLANES - 1, axis=0))
            cw = cw_ref[:, col0:col0 + ff_chunk]
            return (up_ref[0:tm, :] * cw[0:1, :]
                    + up_ref[body, :] * cw[1:2, :]
                    + up_ref[2 * SUBLANES:2 * SUBLANES + tm, :] * cw[2:3, :]
                    + cb_ref[:, col0:col0 + ff_chunk])

        for c in range(d_ff // ff_chunk):
            val = conv(c * ff_chunk)
            gt = conv(d_ff + c * ff_chunk)
            act_ref[st, :, c * ff_chunk:(c + 1) * ff_chunk] = ((gt * jax.nn.sigmoid(gt)) * val).astype(BF16)

        put(body, mod_ref[5:6, :] * jnp.dot(act_ref[st], wdn_ref[...], preferred_element_type=F32))
        for r in range(SUBLANES):
            rows = pl.ds(SUBLANES + r, seg, stride=SUBLANES)
            xo = x_ref[t0 + r * seg:t0 + (r + 1) * seg, :] + jnp.concatenate(
                [hp_ref[st, j, rows, :] for j in range(slabs)], axis=1)
            if final_norm:
                xo = (xo * lax.rsqrt(jnp.mean(xo * xo, axis=-1, keepdims=True) + EPS)) * fw_ref[...]
            o_ref[t0 + r * seg:t0 + (r + 1) * seg, :] = xo


def _ffn(x, mod, norm_w, w_up, conv_w, conv_b, w_down, final_w, layer, final_norm, tm, tiles_per_step):
    B, S, D = x.shape
    d_ff = w_down.shape[-2]
    ff_chunk = _ff_chunk(d_ff)
    rows = tm * tiles_per_step
    halo_blocks = rows // SUBLANES
    n_halo = S // SUBLANES
    row_spec = pl.BlockSpec((None, rows, D), lambda b, s: (b, s, 0))
    final_w = final_w.reshape(1, D)
    return pl.pallas_call(
        functools.partial(_ffn_kernel, tm=tm, d_ff=d_ff, ff_chunk=ff_chunk, final_norm=final_norm),
        grid=(B, S // rows),
        in_specs=[
            row_spec,
            pl.BlockSpec((None, SUBLANES, D), lambda b, s: (b, jnp.maximum(s * halo_blocks - 1, 0), 0)),
            pl.BlockSpec((None, SUBLANES, D),
                         lambda b, s: (b, jnp.minimum((s + 1) * halo_blocks, n_halo - 1), 0)),
            pl.BlockSpec((None, 6, D), lambda b, s: (b, 0, 0)),
            _resident(norm_w, layer),
            _resident(w_up, layer), _resident(conv_w, layer), _resident(conv_b, layer), _resident(w_down, layer),
            _resident(final_w),
        ],
        out_specs=row_spec,
        out_shape=jax.ShapeDtypeStruct((B, S, D), F32),
        scratch_shapes=[
            pltpu.VMEM((tiles_per_step, D // LANES, tm + 2 * SUBLANES, LANES), F32),
            pltpu.VMEM((tm + 2 * SUBLANES, ff_chunk), F32),
            pltpu.VMEM((tiles_per_step, tm, d_ff), BF16),
        ],
        compiler_params=_params(("arbitrary", "arbitrary")),
        name="convffn",
    )(x, x, x, mod, norm_w, w_up, conv_w, conv_b, w_down, final_w)


def _rope_tables(seq):
    half = HEAD_DIM // 2
    inv_freq = ROPE_THETA ** (-jnp.arange(0, half, dtype=F32) / half)
    ang = jnp.arange(seq, dtype=F32)[:, None] * inv_freq[None, :]
    cos, sin = jnp.cos(ang), jnp.sin(ang)
    reps = LANES // HEAD_DIM
    cos_t = jnp.tile(jnp.concatenate([cos, cos], axis=1), (1, reps))
    sin_t = jnp.tile(jnp.concatenate([-sin, sin], axis=1), (1, reps))
    return cos_t, sin_t


def _ff_chunk(d_ff):
    return MXU_DIM if d_ff % MXU_DIM == 0 else d_ff


def _pick(size, pref):
    return pref if size % pref == 0 else size


def kernel(x, c, norm1_w, norm2_w, ada_w, ada_b, w_in, lam_q1, lam_k1, lam_q2, lam_k2, subln_w, w_attn_out, ssm_lam_re, ssm_lam_im, ssm_log_dt, ssm_b_re, ssm_b_im, ssm_c_re, ssm_c_im, ssm_d, w_glu_a, w_glu_b, w_out, w_up, conv_w, conv_b, w_down, final_w):
    B, S, D = x.shape
    depth = ada_w.shape[0]
    ssm_width = ssm_d.shape[1]
    assert B == SUBLANES, "the S5 scan maps the batch onto the 8 sublanes of a vreg"
    assert w_in.shape[2] == 3 * N_HEADS * HEAD_WIDTH + ssm_width + 2 * D
    tm = _pick(S, 512)
    tq = _pick(S, 1024)
    tk = _pick(S, 1024)
    steps = _pick(S, 64)
    row_tiles = 2 if S % (2 * tm) == 0 else 1
    q_tiles = 2 if S % (2 * tq) == 0 else 1

    cos_t, sin_t = _rope_tables(S)
    mod_all = _adaln(c, ada_w, ada_b).reshape(depth, B, 6, D)
    w_in_b, w_up_b, w_down_b = w_in.astype(BF16), w_up.astype(BF16), w_down.astype(BF16)
    w_attn_out_b, w_out_b = w_attn_out.astype(BF16), w_out.astype(BF16)
    w_glu_b16 = jnp.concatenate([w_glu_a, w_glu_b], axis=2).astype(BF16)
    norm1, norm2 = norm1_w.reshape(depth, 1, D), norm2_w.reshape(depth, 1, D)
    sub_w = subln_w.reshape(depth, 1, HEAD_WIDTH)
    d_skip = ssm_d.reshape(depth, 1, ssm_width)
    conv_b3 = conv_b.reshape(depth, 1, -1)
    b_mat, c_mat, a_bar = _s5_discretize(ssm_lam_re, ssm_lam_im, ssm_log_dt, ssm_b_re, ssm_b_im,
                                         ssm_c_re, ssm_c_im, B)
    lam_all = (jnp.exp(jnp.sum(lam_q1 * lam_k1, axis=-1)) - jnp.exp(jnp.sum(lam_q2 * lam_k2, axis=-1))).astype(F32)

    for l in range(depth):
        lambda_init = 0.8 - 0.6 * math.exp(-0.3 * l)
        mod = mod_all[l]
        q, k, v, u, ga, gs = _inproj(x, mod, norm1, w_in_b, l, cos_t, sin_t, ssm_width, tm, row_tiles)
        o_attn = _attention(q, k, v, lam_all[l] + lambda_init, sub_w, l, 1.0 - lambda_init, tq, tk, q_tiles)
        yf, yb = _s5(u, b_mat, c_mat, a_bar, l, steps)
        x = _merge(x, mod, o_attn, yf, yb, u, d_skip, ga, gs, w_attn_out_b, w_glu_b16, w_out_b, l, tm, row_tiles)
        x = _ffn(x, mod, norm2, w_up_b, conv_w, conv_b3, w_down_b, final_w, l, l == depth - 1, tm, row_tiles)
    return x
```

```python
import functools
import math

import jax
import jax.numpy as jnp
from jax import lax
from jax.experimental import pallas as pl
from jax.experimental.pallas import tpu as pltpu

N_HEADS = 8
HEAD_DIM = 64
HEAD_WIDTH = 2 * HEAD_DIM
ROPE_THETA = 10000.0
SSM_GROUP = 16
SSM_STATE = 64
CONV_WIDTH = 3
EPS = 1e-6
SUBLN_EPS = 1e-5
LOG2_E = math.log2(math.e)

LANES = 128
SUBLANES = 8
MXU_DIM = 256
VMEM_LIMIT_BYTES = 56 * 1024 * 1024

BF16 = jnp.bfloat16
F32 = jnp.float32


def _params(semantics):
    return pltpu.CompilerParams(dimension_semantics=semantics, vmem_limit_bytes=VMEM_LIMIT_BYTES)


def _modulated_rmsnorm(x, w, scale, shift):
    y = x * lax.rsqrt(jnp.mean(x * x, axis=-1, keepdims=True) + EPS)
    return (y * w) * (1.0 + scale) + shift


def _adaln_kernel(c_ref, w_ref, b_ref, o_ref):
    c = c_ref[...]
    c = c * jax.nn.sigmoid(c)
    w = w_ref[...]
    c_hi = c.astype(BF16)
    c_lo = (c - c_hi.astype(F32)).astype(BF16)
    w_hi = w.astype(BF16)
    w_lo = (w - w_hi.astype(F32)).astype(BF16)
    acc = jnp.dot(c_hi, w_hi, preferred_element_type=F32)
    acc += jnp.dot(c_hi, w_lo, preferred_element_type=F32)
    acc += jnp.dot(c_lo, w_hi, preferred_element_type=F32)
    o_ref[...] = acc + b_ref[...]


def _adaln(c, ada_w, ada_b):
    L, D, N = ada_w.shape
    B = c.shape[0]
    tn = 1536 if N % 1536 == 0 else N
    return pl.pallas_call(
        _adaln_kernel,
        grid=(L, N // tn),
        in_specs=[
            pl.BlockSpec((B, D), lambda l, j: (0, 0)),
            pl.BlockSpec((None, D, tn), lambda l, j: (l, 0, j)),
            pl.BlockSpec((None, 1, tn), lambda l, j: (l, 0, j)),
        ],
        out_specs=pl.BlockSpec((None, B, tn), lambda l, j: (l, 0, j)),
        out_shape=jax.ShapeDtypeStruct((L, B, N), F32),
        compiler_params=_params(("arbitrary", "arbitrary")),
        name="adaln",
    )(c, ada_w, ada_b.reshape(L, 1, N))


def _inproj_kernel(x_ref, mod_ref, nw_ref, w_ref, cos_ref, sin_ref,
                   q_ref, k_ref, v_ref, u_ref, ga_ref, gs_ref, *, tm, d_model, ssm_width):
    qk_width = N_HEADS * HEAD_WIDTH
    first_half = (lax.broadcasted_iota(jnp.int32, (tm, LANES), 1) % HEAD_DIM) < (HEAD_DIM // 2)

    for st in range(x_ref.shape[0] // tm):
        rows = slice(st * tm, (st + 1) * tm)
        h = _modulated_rmsnorm(x_ref[rows, :], nw_ref[...], mod_ref[1:2, :], mod_ref[0:1, :]).astype(BF16)
        cos = cos_ref[rows, :]
        sin = sin_ref[rows, :]

        def rope_heads(col0, out_ref, scale):
            for pair in range(N_HEADS // 2):
                c0 = col0 + pair * 2 * HEAD_WIDTH
                t2 = jnp.dot(h, w_ref[:, c0:c0 + 2 * HEAD_WIDTH], preferred_element_type=F32)
                for i in range(2):
                    t = t2[:, i * HEAD_WIDTH:(i + 1) * HEAD_WIDTH]
                    partner = jnp.where(first_half,
                                        pltpu.roll(t, HEAD_WIDTH - HEAD_DIM // 2, axis=1),
                                        pltpu.roll(t, HEAD_DIM // 2, axis=1))
                    r = t * cos + partner * sin
                    if scale != 1.0:
                        r = r * scale
                    out_ref[2 * pair + i, rows, :] = r.astype(BF16)

        rope_heads(0, q_ref, LOG2_E / math.sqrt(HEAD_DIM))
        rope_heads(qk_width, k_ref, 1.0)
        for pair in range(N_HEADS // 2):
            c0 = 2 * qk_width + pair * 2 * HEAD_WIDTH
            t2 = jnp.dot(h, w_ref[:, c0:c0 + 2 * HEAD_WIDTH], preferred_element_type=F32).astype(BF16)
            v_ref[2 * pair, rows, :] = t2[:, :HEAD_WIDTH]
            v_ref[2 * pair + 1, rows, :] = t2[:, HEAD_WIDTH:]
        c0 = 3 * qk_width
        u_ref[rows, :] = jnp.dot(h, w_ref[:, c0:c0 + ssm_width], preferred_element_type=F32).astype(BF16)
        c0 += ssm_width
        ga_ref[rows, :] = jax.nn.sigmoid(
            jnp.dot(h, w_ref[:, c0:c0 + d_model], preferred_element_type=F32)).astype(BF16)
        c0 += d_model
        gs_ref[rows, :] = jax.nn.sigmoid(
            jnp.dot(h, w_ref[:, c0:c0 + d_model], preferred_element_type=F32)).astype(BF16)


def _resident(arr, *lead):
    shape = (None,) * len(lead) + arr.shape[len(lead):]
    index = tuple(lead) + (0,) * (arr.ndim - len(lead))
    return pl.BlockSpec(shape, lambda *_: index, pipeline_mode=pl.Buffered(1))


def _inproj(x, mod, norm_w, w_in, layer, cos, sin, ssm_width, tm, tiles_per_step):
    B, S, D = x.shape
    rows = tm * tiles_per_step
    head_shape = jax.ShapeDtypeStruct((B, N_HEADS, S, HEAD_WIDTH), BF16)
    head_spec = pl.BlockSpec((None, N_HEADS, rows, HEAD_WIDTH), lambda b, s: (b, 0, s, 0))
    row_spec = pl.BlockSpec((None, rows, D), lambda b, s: (b, s, 0))
    return pl.pallas_call(
        functools.partial(_inproj_kernel, tm=tm, d_model=D, ssm_width=ssm_width),
        grid=(B, S // rows),
        in_specs=[
            row_spec,
            pl.BlockSpec((None, 6, D), lambda b, s: (b, 0, 0)),
            _resident(norm_w, layer),
            _resident(w_in, layer),
            pl.BlockSpec((rows, LANES), lambda b, s: (s, 0)),
            pl.BlockSpec((rows, LANES), lambda b, s: (s, 0)),
        ],
        out_specs=[
            head_spec, head_spec, head_spec,
            pl.BlockSpec((None, rows, ssm_width), lambda b, s: (b, s, 0)),
            row_spec, row_spec,
        ],
        out_shape=[
            head_shape, head_shape, head_shape,
            jax.ShapeDtypeStruct((B, S, ssm_width), BF16),
            jax.ShapeDtypeStruct((B, S, D), BF16),
            jax.ShapeDtypeStruct((B, S, D), BF16),
        ],
        compiler_params=_params(("arbitrary", "arbitrary")),
        name="inproj",
    )(x, mod, norm_w, w_in, cos, sin)


def _attn_kernel(lam_ref, q_ref, k_ref, v_ref, sw_ref, o_ref, vaug_ref, *, tq, tk, out_scale):
    S = k_ref.shape[0]

    @pl.when(pl.program_id(2) == 0)
    def _():
        vaug_ref[:, :HEAD_WIDTH] = v_ref[...]
        vaug_ref[:, HEAD_WIDTH:] = jnp.ones((S, MXU_DIM - HEAD_WIDTH), BF16)

    n_tiles = q_ref.shape[0] // tq
    for qt in range(n_tiles):
        rows = slice(qt * tq, (qt + 1) * tq)
        q = q_ref[rows, :]
        lane = lax.broadcasted_iota(jnp.int32, q.shape, 1)
        zero = jnp.zeros_like(q)
        subs = []
        for i, q_sub in enumerate((jnp.where(lane < HEAD_DIM, q, zero), jnp.where(lane >= HEAD_DIM, q, zero))):
            m = jnp.full((tq, 1), -jnp.inf, F32)
            acc = jnp.zeros((tq, MXU_DIM), F32)
            edges = list(range(0, S + 1, tk))
            if qt == 0 and i == 0 and tk >= 4 * LANES:
                edges.insert(1, tk // 4)
            if qt == n_tiles - 1 and i == 1 and tk >= 4 * LANES:
                edges.insert(-1, S - tk // 4)
            for k0, k1 in zip(edges[:-1], edges[1:]):
                kc = k_ref[k0:k1, :]
                s = lax.dot_general(q_sub, kc, (((1,), (1,)), ((), ())), preferred_element_type=F32)
                m_new = jnp.maximum(m, jnp.max(s, axis=1, keepdims=True))
                alpha = jnp.exp2(m - m_new)
                p = jnp.exp2(s - m_new).astype(BF16)
                acc = acc * alpha + jnp.dot(p, vaug_ref[k0:k1, :], preferred_element_type=F32)
                m = m_new
            subs.append(acc[:, :HEAD_WIDTH] / acc[:, HEAD_WIDTH:])
        o = subs[0] - lam_ref[0] * subs[1]
        o = o * lax.rsqrt(jnp.mean(o * o, axis=-1, keepdims=True) + SUBLN_EPS)
        o_ref[rows, :] = ((o * sw_ref[...]) * out_scale).astype(BF16)


def _attention(q, k, v, lam, sub_w, layer, out_scale, tq, tk, tiles_per_step):
    B, H, S, W = q.shape
    rows = tq * tiles_per_step
    kv_spec = pl.BlockSpec((None, None, S, W), lambda b, h, i: (b, h, 0, 0))
    return pl.pallas_call(
        functools.partial(_attn_kernel, tq=tq, tk=tk, out_scale=out_scale),
        grid=(B, H, S // rows),
        in_specs=[
            pl.BlockSpec(memory_space=pltpu.SMEM),
            pl.BlockSpec((None, None, rows, W), lambda b, h, i: (b, h, i, 0)),
            kv_spec, kv_spec,
            _resident(sub_w, layer),
        ],
        out_specs=pl.BlockSpec((None, rows, W), lambda b, h, i: (b, i, h)),
        out_shape=jax.ShapeDtypeStruct((B, S, H * W), BF16),
        scratch_shapes=[pltpu.VMEM((S, MXU_DIM), BF16)],
        compiler_params=_params(("arbitrary", "arbitrary", "arbitrary")),
        name="diff_attn",
    )(lam.reshape(1), q, k, v, sub_w)


def _s5_discretize(lam_re, lam_im, log_dt, b_re, b_im, c_re, c_im, batch):
    lead = lam_re.shape[:-2]
    G, P = lam_re.shape[-2:]
    N = b_re.shape[-1]
    dt = jnp.exp(log_dt)[..., None]
    mag = jnp.exp(lam_re * dt)
    ang = lam_im * dt
    abar_re = mag * jnp.cos(ang)
    abar_im = mag * jnp.sin(ang)
    den = lam_re * lam_re + lam_im * lam_im
    nr = abar_re - 1.0
    ni = abar_im
    f_re = (nr * lam_re + ni * lam_im) / den
    f_im = (ni * lam_re - nr * lam_im) / den
    bb_re = f_re[..., None] * b_re - f_im[..., None] * b_im
    bb_im = f_re[..., None] * b_im + f_im[..., None] * b_re
    gh = G // 2
    nl = len(lead)

    def block_diag(m, rows, cols):
        m = m.reshape(lead + (2, gh, cols, rows))
        wide = jnp.moveaxis(m, -1, nl + 1).reshape(lead + (2, 1, rows, gh * cols))
        on_diag = (jnp.arange(gh)[:, None, None] == (jnp.arange(gh * cols) // cols)[None, None, :])
        return jnp.where(on_diag, wide, 0.0).reshape(lead + (2, gh * rows, gh * cols))

    def in_block(bb):
        return block_diag(bb, N, P)

    def out_block(cc):
        return block_diag(cc, P, N)

    b_mat = jnp.concatenate([in_block(bb_re), in_block(bb_im)], axis=-1)
    c_mat = jnp.concatenate([out_block(c_re), out_block(-c_im)], axis=-2)
    a = jnp.stack([abar_re.reshape(lead + (G * P,)), abar_im.reshape(lead + (G * P,))], axis=-2)
    a = jnp.broadcast_to(a[..., None, :], lead + (2, batch, G * P))
    return b_mat.astype(BF16), c_mat.astype(BF16), a


def _s5_kernel(uf_ref, ub_ref, bf_ref, bb_ref, cf_ref, cb_ref, af_ref, ab_ref,
               yf_ref, yb_ref, tm_ref, xf_ref, xb_ref, carf_ref, carb_ref, *, steps, batch, lane_chunk):
    half_in = uf_ref.shape[2] // 2
    half_state = af_ref.shape[2] // 2

    @pl.when(pl.program_id(0) == 0)
    def _():
        carf_ref[...] = jnp.zeros_like(carf_ref)
        carb_ref[...] = jnp.zeros_like(carb_ref)

    slabs = half_in // LANES

    for u_ref, b_ref, x_ref in ((uf_ref, bf_ref, xf_ref), (ub_ref, bb_ref, xb_ref)):
        for b in range(batch):
            ub = u_ref[b].astype(F32)
            for j in range(2 * slabs):
                tm_ref[j, pl.ds(b, steps, stride=batch), :] = ub[:, j * LANES:(j + 1) * LANES]
        for h in range(2):
            cols = slice(2 * half_state * h, 2 * half_state * (h + 1))
            lhs = jnp.concatenate([tm_ref[h * slabs + j] for j in range(slabs)], axis=1).astype(BF16)
            x_ref[:, cols] = jnp.dot(lhs, b_ref[h], preferred_element_type=F32)

    for h in range(2):
        for j in range(half_state // lane_chunk):
            re = slice(2 * half_state * h + lane_chunk * j, 2 * half_state * h + lane_chunk * (j + 1))
            im = slice(re.start + half_state, re.stop + half_state)
            ac = slice(half_state * h + lane_chunk * j, half_state * h + lane_chunk * (j + 1))
            arf, aif = af_ref[0, :, ac], af_ref[1, :, ac]
            arb, aib = ab_ref[0, :, ac], ab_ref[1, :, ac]

            fr, fi, br, bi = carf_ref[:, re], carf_ref[:, im], carb_ref[:, re], carb_ref[:, im]
            for t in range(steps):
                rf = slice(t * batch, (t + 1) * batch)
                rb = slice((steps - 1 - t) * batch, (steps - t) * batch)
                fr, fi = arf * fr - aif * fi + xf_ref[rf, re], arf * fi + aif * fr + xf_ref[rf, im]
                br, bi = arb * br - aib * bi + xb_ref[rb, re], arb * bi + aib * br + xb_ref[rb, im]
                xf_ref[rf, re] = fr
                xf_ref[rf, im] = fi
                xb_ref[rb, re] = br
                xb_ref[rb, im] = bi
            carf_ref[:, re] = fr
            carf_ref[:, im] = fi
            carb_ref[:, re] = br
            carb_ref[:, im] = bi

    for x_ref, c_ref, y_ref in ((xf_ref, cf_ref, yf_ref), (xb_ref, cb_ref, yb_ref)):
        for h in range(2):
            cols = slice(2 * half_state * h, 2 * half_state * (h + 1))
            y = jnp.dot(x_ref[:, cols].astype(BF16), c_ref[h], preferred_element_type=F32)
            for j in range(slabs):
                tm_ref[h * slabs + j] = y[:, j * LANES:(j + 1) * LANES]
        for b in range(batch):
            for j in range(2 * slabs):
                y_ref[b, :, j * LANES:(j + 1) * LANES] = tm_ref[j, pl.ds(b, steps, stride=batch), :].astype(BF16)


def _s5(u, b_mat, c_mat, a, layer, steps):
    batch, S, W = u.shape
    rows = steps * batch
    n = S // steps
    n_state = a.shape[-1]
    return pl.pallas_call(
        functools.partial(_s5_kernel, steps=steps, batch=batch, lane_chunk=512),
        grid=(n,),
        in_specs=[
            pl.BlockSpec((batch, steps, W), lambda i: (0, i, 0)),
            pl.BlockSpec((batch, steps, W), lambda i: (0, n - 1 - i, 0)),
            _resident(b_mat, layer, 0), _resident(b_mat, layer, 1),
            _resident(c_mat, layer, 0), _resident(c_mat, layer, 1),
            _resident(a, layer, 0), _resident(a, layer, 1),
        ],
        out_specs=[
            pl.BlockSpec((batch, steps, W), lambda i: (0, i, 0)),
            pl.BlockSpec((batch, steps, W), lambda i: (0, n - 1 - i, 0)),
        ],
        out_shape=[jax.ShapeDtypeStruct((batch, S, W), BF16)] * 2,
        scratch_shapes=[
            pltpu.VMEM((W // LANES, rows, LANES), F32),
            pltpu.VMEM((rows, 2 * n_state), F32), pltpu.VMEM((rows, 2 * n_state), F32),
            pltpu.VMEM((batch, 2 * n_state), F32), pltpu.VMEM((batch, 2 * n_state), F32),
        ],
        compiler_params=_params(("arbitrary",)),
        name="s5_scan",
    )(u, u, b_mat, b_mat, c_mat, c_mat, a, a)


def _gelu_tanh(y):
    return 0.5 * y * (1.0 + jnp.tanh(math.sqrt(2.0 / math.pi) * (y + 0.044715 * (y * y * y))))


def _merge_kernel(x_ref, mod_ref, oa_ref, yf_ref, yb_ref, u_ref, d_ref, ga_ref, gs_ref,
                  wao_ref, wglu_ref, wout_ref, o_ref, *, tm, d_model):
    for st in range(x_ref.shape[0] // tm):
        rows = slice(st * tm, (st + 1) * tm)
        y = (yf_ref[rows, :].astype(F32) + yb_ref[rows, :].astype(F32)
             + d_ref[...] * u_ref[rows, :].astype(F32))
        y = _gelu_tanh(y).astype(BF16)
        ab = jnp.dot(y, wglu_ref[...], preferred_element_type=F32)
        o_ssm = ab[:, :d_model] * jax.nn.sigmoid(ab[:, d_model:])
        o_attn = jnp.dot(oa_ref[rows, :], wao_ref[...], preferred_element_type=F32)
        merged = ga_ref[rows, :].astype(F32) * o_attn + gs_ref[rows, :].astype(F32) * o_ssm
        out = jnp.dot(merged.astype(BF16), wout_ref[...], preferred_element_type=F32)
        o_ref[rows, :] = x_ref[rows, :] + mod_ref[2:3, :] * out


def _merge(x, mod, o_attn, yf, yb, u, ssm_d, ga, gs, w_attn_out, w_glu, w_out, layer, tm, tiles_per_step):
    B, S, D = x.shape
    W = ssm_d.shape[-1]
    rows = tm * tiles_per_step
    row_spec = pl.BlockSpec((None, rows, D), lambda b, s: (b, s, 0))
    ssm_spec = pl.BlockSpec((None, rows, W), lambda b, s: (b, s, 0))
    return pl.pallas_call(
        functools.partial(_merge_kernel, tm=tm, d_model=D),
        grid=(B, S // rows),
        in_specs=[
            row_spec,
            pl.BlockSpec((None, 6, D), lambda b, s: (b, 0, 0)),
            row_spec, ssm_spec, ssm_spec, ssm_spec,
            _resident(ssm_d, layer),
            row_spec, row_spec,
            _resident(w_attn_out, layer), _resident(w_glu, layer), _resident(w_out, layer),
        ],
        out_specs=row_spec,
        out_shape=jax.ShapeDtypeStruct((B, S, D), F32),
        compiler_params=_params(("arbitrary", "arbitrary")),
        name="merge",
    )(x, mod, o_attn, yf, yb, u, ssm_d, ga, gs, w_attn_out, w_glu, w_out)


def _ffn_kernel(x_ref, xp_ref, xn_ref, mod_ref, nw_ref, wup_ref, cw_ref, cb_ref, wdn_ref, fw_ref,
                o_ref, hp_ref, up_ref, act_ref, *, tm, d_ff, ff_chunk, final_norm):
    n_sub = x_ref.shape[0] // tm
    seg = tm // SUBLANES
    body = slice(SUBLANES, SUBLANES + tm)
    si = pl.program_id(1)
    keep_prev = (si > 0).astype(F32)
    keep_next = (si < pl.num_programs(1) - 1).astype(F32)
    nw, scale, shift = nw_ref[...], mod_ref[4:5, :], mod_ref[3:4, :]
    slabs = hp_ref.shape[1]
    sub = lax.broadcasted_iota(jnp.int32, (SUBLANES, ff_chunk), 0)

    for st in range(n_sub):
        t0 = st * tm

        def put(rows, val):
            for j in range(slabs):
                hp_ref[st, j, rows, :] = val[:, j * LANES:(j + 1) * LANES]

        if st == 0:
            h_prev = _modulated_rmsnorm(xp_ref[...], nw, scale, shift) * keep_prev
        else:
            h_prev = _modulated_rmsnorm(x_ref[t0 - SUBLANES:t0, :], nw, scale, shift)
        if st == n_sub - 1:
            h_next = _modulated_rmsnorm(xn_ref[...], nw, scale, shift) * keep_next
        else:
            h_next = _modulated_rmsnorm(x_ref[t0 + tm:t0 + tm + SUBLANES, :], nw, scale, shift)
        put(slice(0, SUBLANES), h_prev)
        put(slice(SUBLANES + tm, 2 * SUBLANES + tm), h_next)
        for r in range(SUBLANES):
            put(pl.ds(SUBLANES + r, seg, stride=SUBLANES),
                _modulated_rmsnorm(x_ref[t0 + r * seg:t0 + (r + 1) * seg, :], nw, scale, shift))
        h = jnp.concatenate([hp_ref[st, j] for j in range(slabs)], axis=1).astype(BF16)

        def conv(col0):
            up_ref[...] = jnp.dot(h, wup_ref[:, col0:col0 + ff_chunk], preferred_element_type=F32)
            up_ref[0:SUBLANES, :] = jnp.where(
                sub == 0, pltpu.roll(up_ref[0:SUBLANES, :], 1, axis=0),
                pltpu.roll(up_ref[tm:SUBLANES + tm, :], 1, axis=0))
            up_ref[SUBLANES + tm:, :] = jnp.where(
                sub == SUBLANES - 1, pltpu.roll(up_ref[SUBLANES + tm:, :], SUBLANES - 1, axis=0),
                pltpu.roll(up_ref[SUBLANES:2 * SUBLANES, :], SUBLANES - 1, axis=0))
            cw = cw_ref[:, col0:col0 + ff_chunk]
            return (up_ref[0:tm, :] * cw[0:1, :]
                    + up_ref[body, :] * cw[1:2, :]
                    + up_ref[2 * SUBLANES:2 * SUBLANES + tm, :] * cw[2:3, :]
                    + cb_ref[:, col0:col0 + ff_chunk])

        for c in range(d_ff // ff_chunk):
            val = conv(c * ff_chunk)
            gt = conv(d_ff + c * ff_chunk)
            act_ref[st, :, c * ff_chunk:(c + 1) * ff_chunk] = ((gt * jax.nn.sigmoid(gt)) * val).astype(BF16)

        put(body, mod_ref[5:6, :] * jnp.dot(act_ref[st], wdn_ref[...], preferred_element_type=F32))
        for r in range(SUBLANES):
            rows = pl.ds(SUBLANES + r, seg, stride=SUBLANES)
            xo = x_ref[t0 + r * seg:t0 + (r + 1) * seg, :] + jnp.concatenate(
                [hp_ref[st, j, rows, :] for j in range(slabs)], axis=1)
            if final_norm:
                xo = (xo * lax.rsqrt(jnp.mean(xo * xo, axis=-1, keepdims=True) + EPS)) * fw_ref[...]
            o_ref[t0 + r * seg:t0 + (r + 1) * seg, :] = xo


def _ffn(x, mod, norm_w, w_up, conv_w, conv_b, w_down, final_w, layer, final_norm, tm, tiles_per_step):
    B, S, D = x.shape
    d_ff = w_down.shape[-2]
    ff_chunk = _ff_chunk(d_ff)
    rows = tm * tiles_per_step
    halo_blocks = rows // SUBLANES
    n_halo = S // SUBLANES
    row_spec = pl.BlockSpec((None, rows, D), lambda b, s: (b, s, 0))
    final_w = final_w.reshape(1, D)
    return pl.pallas_call(
        functools.partial(_ffn_kernel, tm=tm, d_ff=d_ff, ff_chunk=ff_chunk, final_norm=final_norm),
        grid=(B, S // rows),
        in_specs=[
            row_spec,
            pl.BlockSpec((None, SUBLANES, D), lambda b, s: (b, jnp.maximum(s * halo_blocks - 1, 0), 0)),
            pl.BlockSpec((None, SUBLANES, D),
                         lambda b, s: (b, jnp.minimum((s + 1) * halo_blocks, n_halo - 1), 0)),
            pl.BlockSpec((None, 6, D), lambda b, s: (b, 0, 0)),
            _resident(norm_w, layer),
            _resident(w_up, layer), _resident(conv_w, layer), _resident(conv_b, layer), _resident(w_down, layer),
            _resident(final_w),
        ],
        out_specs=row_spec,
        out_shape=jax.ShapeDtypeStruct((B, S, D), F32),
        scratch_shapes=[
            pltpu.VMEM((tiles_per_step, D // LANES, tm + 2 * SUBLANES, LANES), F32),
            pltpu.VMEM((tm + 2 * SUBLANES, ff_chunk), F32),
            pltpu.VMEM((tiles_per_step, tm, d_ff), BF16),
        ],
        compiler_params=_params(("arbitrary", "arbitrary")),
        name="convffn",
    )(x, x, x, mod, norm_w, w_up, conv_w, conv_b, w_down, final_w)


def _rope_tables(seq):
    half = HEAD_DIM // 2
    inv_freq = ROPE_THETA ** (-jnp.arange(0, half, dtype=F32) / half)
    ang = jnp.arange(seq, dtype=F32)[:, None] * inv_freq[None, :]
    cos, sin = jnp.cos(ang), jnp.sin(ang)
    reps = LANES // HEAD_DIM
    cos_t = jnp.tile(jnp.concatenate([cos, cos], axis=1), (1, reps))
    sin_t = jnp.tile(jnp.concatenate([-sin, sin], axis=1), (1, reps))
    return cos_t, sin_t


def _ff_chunk(d_ff):
    return MXU_DIM if d_ff % MXU_DIM == 0 else d_ff


def _pick(size, pref):
    return pref if size % pref == 0 else size


def kernel(x, c, norm1_w, norm2_w, ada_w, ada_b, w_in, lam_q1, lam_k1, lam_q2, lam_k2, subln_w, w_attn_out, ssm_lam_re, ssm_lam_im, ssm_log_dt, ssm_b_re, ssm_b_im, ssm_c_re, ssm_c_im, ssm_d, w_glu_a, w_glu_b, w_out, w_up, conv_w, conv_b, w_down, final_w):
    B, S, D = x.shape
    depth = ada_w.shape[0]
    ssm_width = ssm_d.shape[1]
    assert B == SUBLANES, "the S5 scan maps the batch onto the 8 sublanes of a vreg"
    assert w_in.shape[2] == 3 * N_HEADS * HEAD_WIDTH + ssm_width + 2 * D
    tm = _pick(S, 512)
    tq = _pick(S, 1024)
    tk = _pick(S, 1024)
    steps = _pick(S, 64)
    row_tiles = 2 if S % (2 * tm) == 0 else 1
    q_tiles = 2 if S % (2 * tq) == 0 else 1

    cos_t, sin_t = _rope_tables(S)
    mod_all = _adaln(c, ada_w, ada_b).reshape(depth, B, 6, D)
    w_in_b, w_up_b, w_down_b = w_in.astype(BF16), w_up.astype(BF16), w_down.astype(BF16)
    w_attn_out_b, w_out_b = w_attn_out.astype(BF16), w_out.astype(BF16)
    w_glu_b16 = jnp.concatenate([w_glu_a, w_glu_b], axis=2).astype(BF16)
    norm1, norm2 = norm1_w.reshape(depth, 1, D), norm2_w.reshape(depth, 1, D)
    sub_w = subln_w.reshape(depth, 1, HEAD_WIDTH)
    d_skip = ssm_d.reshape(depth, 1, ssm_width)
    conv_b3 = conv_b.reshape(depth, 1, -1)
    b_mat, c_mat, a_bar = _s5_discretize(ssm_lam_re, ssm_lam_im, ssm_log_dt, ssm_b_re, ssm_b_im,
                                         ssm_c_re, ssm_c_im, B)
    lam_all = (jnp.exp(jnp.sum(lam_q1 * lam_k1, axis=-1)) - jnp.exp(jnp.sum(lam_q2 * lam_k2, axis=-1))).astype(F32)

    for l in range(depth):
        lambda_init = 0.8 - 0.6 * math.exp(-0.3 * l)
        mod = mod_all[l]
        q, k, v, u, ga, gs = _inproj(x, mod, norm1, w_in_b, l, cos_t, sin_t, ssm_width, tm, row_tiles)
        o_attn = _attention(q, k, v, lam_all[l] + lambda_init, sub_w, l, 1.0 - lambda_init, tq, tk, q_tiles)
        yf, yb = _s5(u, b_mat, c_mat, a_bar, l, steps)
        x = _merge(x, mod, o_attn, yf, yb, u, d_skip, ga, gs, w_attn_out_b, w_glu_b16, w_out_b, l, tm, row_tiles)
        x = _ffn(x, mod, norm2, w_up_b, conv_w, conv_b3, w_down_b, final_w, l, l == depth - 1, tm, row_tiles)
    return x
```
